```python
import jax, jax.numpy as jnp
from jax import lax
import numpy as np

D_MODEL = 1024
BATCH = 4
SEQ = 8192
DEPTH = 4

CHUNK = 64
WINDOW = 128
WIN_CHUNKS = WINDOW // CHUNK
BAND = (WIN_CHUNKS + 1) * CHUNK
HEAD_DIM = 64
ATTN_WIDTH = D_MODEL // 2
N_Q_HEADS = ATTN_WIDTH // HEAD_DIM
N_KV_HEADS = N_Q_HEADS // 4
KV_WIDTH = N_KV_HEADS * HEAD_DIM
LRU_WIDTH = D_MODEL
LRU_BLOCKS = 16
LRU_BLOCK = LRU_WIDTH // LRU_BLOCKS
CONV_WIDTH = 4
LRU_C = 8.0
N_MEM = 256
MEM_HEADS = 4
MEM_WIDTH = D_MODEL // 2
MEM_HEAD_DIM = MEM_WIDTH // MEM_HEADS
N_BRANCH = 3
D_FF = 11 * D_MODEL // 4
EPS = 1e-6
IN_WIDTH = ATTN_WIDTH + 2 * KV_WIDTH + 2 * LRU_WIDTH + MEM_WIDTH + N_BRANCH * D_MODEL

kernel_name = 'hybrid_swa_rglru_memory_macaron'


def rmsnorm(x, g):
    x32 = x.astype(jnp.float32)
    y = x32 * lax.rsqrt(jnp.mean(x32 * x32, axis=-1, keepdims=True) + EPS)
    return (y * g.astype(jnp.float32)).astype(x.dtype)


def swiglu(h, w_gate, w_up, w_down):
    return (jax.nn.silu(h @ w_gate) * (h @ w_up)) @ w_down


def alibi_slopes(n):
    return jnp.asarray(np.array([2.0 ** (-8.0 * (i + 1) / n) for i in range(n)], dtype=np.float32))


def window_attention(q, k, v, sinks):
    B, S = q.shape[0], q.shape[1]
    nc = S // CHUNK
    G = N_Q_HEADS // N_KV_HEADS
    pad = WIN_CHUNKS * CHUNK
    kp = jnp.pad(k, ((0, 0), (pad, 0), (0, 0), (0, 0))).reshape(B, nc + WIN_CHUNKS, CHUNK, N_KV_HEADS, HEAD_DIM)
    vp = jnp.pad(v, ((0, 0), (pad, 0), (0, 0), (0, 0))).reshape(B, nc + WIN_CHUNKS, CHUNK, N_KV_HEADS, HEAD_DIM)
    kb = jnp.concatenate([kp[:, j:j + nc] for j in range(WIN_CHUNKS + 1)], axis=2)
    vb = jnp.concatenate([vp[:, j:j + nc] for j in range(WIN_CHUNKS + 1)], axis=2)
    qb = q.reshape(B, nc, CHUNK, N_KV_HEADS, G, HEAD_DIM)
    s = jnp.einsum('bcqkgd,bcskd->bckgqs', qb, kb).astype(jnp.float32) * (HEAD_DIM ** -0.5)
    qi = jnp.arange(CHUNK) + pad
    kj = jnp.arange(BAND)
    dist = jnp.abs(qi[:, None] - kj[None, :]).astype(jnp.float32)
    slopes = alibi_slopes(N_Q_HEADS).reshape(N_KV_HEADS, G)
    s = s - slopes[:, :, None, None] * dist
    key_chunk = jnp.arange(nc)[:, None] - WIN_CHUNKS + kj[None, :] // CHUNK
    valid = key_chunk >= 0
    s = jnp.where(valid[None, :, None, None, None, :], s, -jnp.inf)
    sink = sinks.astype(jnp.float32).reshape(N_KV_HEADS, G)[None, None, :, :, None, None]
    m = jnp.maximum(jnp.max(s, axis=-1, keepdims=True), sink)
    p = jnp.exp(s - m)
    p = p / (jnp.sum(p, axis=-1, keepdims=True) + jnp.exp(sink - m))
    o = jnp.einsum('bckgqs,bcskd->bcqkgd', p.astype(v.dtype), vb)
    return o.reshape(B, S, ATTN_WIDTH)


def rglru_branch(xr, yr, conv_w, conv_b, wa, ba, wx, bx, lam):
    B, S = xr.shape[0], xr.shape[1]
    xc = lax.conv_general_dilated(xr, conv_w[:, None, :].astype(xr.dtype), window_strides=(1,),
                                  padding=[(CONV_WIDTH - 1, 0)],
                                  dimension_numbers=('NWC', 'WIO', 'NWC'),
                                  feature_group_count=LRU_WIDTH) + conv_b
    xb = xc.reshape(B, S, LRU_BLOCKS, LRU_BLOCK)
    r = jax.nn.sigmoid((jnp.einsum('bshi,hij->bshj', xb, wa).reshape(B, S, LRU_WIDTH) + ba).astype(jnp.float32))
    gi = jax.nn.sigmoid((jnp.einsum('bshi,hij->bshj', xb, wx).reshape(B, S, LRU_WIDTH) + bx).astype(jnp.float32))
    log_a = -LRU_C * r * jax.nn.softplus(-lam.astype(jnp.float32))
    a = jnp.exp(log_a)
    mult = jnp.sqrt(jnp.maximum(1.0 - jnp.exp(2.0 * log_a), 0.0))
    mult = jnp.where(jnp.arange(S)[None, :, None] == 0, 1.0, mult)
    b = mult * gi * xc.astype(jnp.float32)

    def combine(left, right):
        al, bl = left
        ar, br = right
        return al * ar, ar * bl + br

    _, h = lax.associative_scan(combine, (a, b), axis=1)
    return h.astype(xr.dtype) * jax.nn.gelu(yr)


def memory_attention(cq, mem_n, w_mem_kv):
    B, S = cq.shape[0], cq.shape[1]
    M = mem_n.shape[1]
    kv = mem_n @ w_mem_kv
    k = kv[..., :MEM_WIDTH].reshape(B, M, MEM_HEADS, MEM_HEAD_DIM)
    v = kv[..., MEM_WIDTH:].reshape(B, M, MEM_HEADS, MEM_HEAD_DIM)
    q = cq.reshape(B, S, MEM_HEADS, MEM_HEAD_DIM)
    s = jnp.einsum('bshd,bmhd->bhsm', q, k).astype(jnp.float32) * (MEM_HEAD_DIM ** -0.5)
    p = jax.nn.softmax(s, axis=-1).astype(v.dtype)
    return jnp.einsum('bhsm,bmhd->bshd', p, v).reshape(B, S, MEM_WIDTH)


def setup_inputs(seed: int = 0) -> dict:
    key = jax.random.key(seed)
    ks = jax.random.split(key, 32)
    f32 = jnp.float32

    def nrm(k, shape, fan_in):
        return jax.random.normal(k, shape, f32) * (fan_in ** -0.5)

    def gain(k, shape):
        return 1.0 + 0.05 * jax.random.normal(k, shape, f32)

    u = jax.random.uniform(ks[20], (DEPTH, LRU_WIDTH), f32, 0.9, 0.999)
    a0 = u ** (1.0 / LRU_C)
    lam = jnp.log(a0) - jnp.log1p(-a0)
    return {
        'x': jax.random.normal(ks[0], (BATCH, SEQ, D_MODEL), f32),
        'mem': jax.random.normal(ks[1], (BATCH, N_MEM, D_MODEL), f32),
        'ffn1_norm': gain(ks[2], (DEPTH, D_MODEL)),
        'ffn1_w_gate': nrm(ks[3], (DEPTH, D_MODEL, D_FF), D_MODEL),
        'ffn1_w_up': nrm(ks[4], (DEPTH, D_MODEL, D_FF), D_MODEL),
        'ffn1_w_down': nrm(ks[5], (DEPTH, D_FF, D_MODEL), D_FF),
        'mix_norm': gain(ks[6], (DEPTH, D_MODEL)),
        'w_in': nrm(ks[7], (DEPTH, D_MODEL, IN_WIDTH), D_MODEL),
        'gate_bias': 0.1 * jax.random.normal(ks[8], (DEPTH, N_BRANCH * D_MODEL), f32),
        'attn_sinks': 0.5 * jax.random.normal(ks[9], (DEPTH, N_Q_HEADS), f32),
        'w_attn_out': nrm(ks[10], (DEPTH, ATTN_WIDTH, D_MODEL), ATTN_WIDTH),
        'conv_w': nrm(ks[11], (DEPTH, CONV_WIDTH, LRU_WIDTH), CONV_WIDTH),
        'conv_b': 0.02 * jax.random.normal(ks[12], (DEPTH, LRU_WIDTH), f32),
        'lru_wa': nrm(ks[13], (DEPTH, LRU_BLOCKS, LRU_BLOCK, LRU_BLOCK), LRU_BLOCK),
        'lru_ba': 0.1 * jax.random.normal(ks[14], (DEPTH, LRU_WIDTH), f32),
        'lru_wx': nrm(ks[15], (DEPTH, LRU_BLOCKS, LRU_BLOCK, LRU_BLOCK), LRU_BLOCK),
        'lru_bx': 0.1 * jax.random.normal(ks[16], (DEPTH, LRU_WIDTH), f32),
        'lru_lambda': lam,
        'w_lru_out': nrm(ks[17], (DEPTH, LRU_WIDTH, D_MODEL), LRU_WIDTH),
        'mem_norm': gain(ks[18], (DEPTH, D_MODEL)),
        'w_mem_kv': nrm(ks[19], (DEPTH, D_MODEL, 2 * MEM_WIDTH), D_MODEL),
        'w_mem_out': nrm(ks[21], (DEPTH, MEM_WIDTH, D_MODEL), MEM_WIDTH),
        'w_out': nrm(ks[22], (DEPTH, D_MODEL, D_MODEL), D_MODEL),
        'ffn2_norm': gain(ks[23], (DEPTH, D_MODEL)),
        'ffn2_w_gate': nrm(ks[24], (DEPTH, D_MODEL, D_FF), D_MODEL),
        'ffn2_w_up': nrm(ks[25], (DEPTH, D_MODEL, D_FF), D_MODEL),
        'ffn2_w_down': nrm(ks[26], (DEPTH, D_FF, D_MODEL), D_FF),
        'final_norm': gain(ks[27], (D_MODEL,)),
    }


def reference(x, mem, ffn1_norm, ffn1_w_gate, ffn1_w_up, ffn1_w_down, mix_norm, w_in, gate_bias,
              attn_sinks, w_attn_out, conv_w, conv_b, lru_wa, lru_ba, lru_wx, lru_bx, lru_lambda,
              w_lru_out, mem_norm, w_mem_kv, w_mem_out, w_out, ffn2_norm, ffn2_w_gate, ffn2_w_up,
              ffn2_w_down, final_norm):
    B, S = x.shape[0], x.shape[1]
    o1 = ATTN_WIDTH
    o2 = o1 + KV_WIDTH
    o3 = o2 + KV_WIDTH
    o4 = o3 + LRU_WIDTH
    o5 = o4 + LRU_WIDTH
    o6 = o5 + MEM_WIDTH
    for l in range(DEPTH):
        x = x + 0.5 * swiglu(rmsnorm(x, ffn1_norm[l]), ffn1_w_gate[l], ffn1_w_up[l], ffn1_w_down[l])
        h = rmsnorm(x, mix_norm[l])
        proj = h @ w_in[l]
        q = proj[..., :o1].reshape(B, S, N_Q_HEADS, HEAD_DIM)
        k = proj[..., o1:o2].reshape(B, S, N_KV_HEADS, HEAD_DIM)
        v = proj[..., o2:o3].reshape(B, S, N_KV_HEADS, HEAD_DIM)
        xr = proj[..., o3:o4]
        yr = proj[..., o4:o5]
        cq = proj[..., o5:o6]
        g = jax.nn.sigmoid((proj[..., o6:] + gate_bias[l]).astype(jnp.float32)).astype(x.dtype)
        g = g.reshape(B, S, N_BRANCH, D_MODEL)
        br_attn = window_attention(q, k, v, attn_sinks[l]) @ w_attn_out[l]
        br_lru = rglru_branch(xr, yr, conv_w[l], conv_b[l], lru_wa[l], lru_ba[l], lru_wx[l], lru_bx[l],
                              lru_lambda[l]) @ w_lru_out[l]
        br_mem = memory_attention(cq, rmsnorm(mem, mem_norm[l]), w_mem_kv[l]) @ w_mem_out[l]
        merged = g[:, :, 0] * br_attn + g[:, :, 1] * br_lru + g[:, :, 2] * br_mem
        x = x + merged @ w_out[l]
        x = x + 0.5 * swiglu(rmsnorm(x, ffn2_norm[l]), ffn2_w_gate[l], ffn2_w_up[l], ffn2_w_down[l])
    return rmsnorm(x, final_norm)
```

```python
import functools
import math

import numpy as np
import jax
import jax.numpy as jnp
from jax import lax
from jax.experimental import pallas as pl
from jax.experimental.pallas import tpu as pltpu

D_MODEL = 1024
CHUNK = 64
WIN_CHUNKS = 2
HEAD_DIM = 64
N_Q_HEADS = 8
N_KV_HEADS = 2
ATTN_WIDTH = N_Q_HEADS * HEAD_DIM
KV_WIDTH = N_KV_HEADS * HEAD_DIM
LRU_WIDTH = D_MODEL
LRU_BLOCKS = 16
LRU_BLOCK = LRU_WIDTH // LRU_BLOCKS
CONV_WIDTH = 4
LRU_C = 8.0
MEM_HEADS = 4
MEM_WIDTH = D_MODEL // 2
MEM_HEAD_DIM = MEM_WIDTH // MEM_HEADS
N_BRANCH = 3
EPS = 1e-6

O_Q = 0
O_KV = O_Q + ATTN_WIDTH
O_XR = O_KV + 2 * KV_WIDTH
O_YR = O_XR + LRU_WIDTH
O_CQ = O_YR + LRU_WIDTH
O_G = O_CQ + MEM_WIDTH
IN_WIDTH = O_G + N_BRANCH * D_MODEL

LANES = 128
SUBLANES = 8
MXU_DIM = 256
VMEM_LIMIT_BYTES = 56 * 1024 * 1024

FFN_ROWS = 512
MIX_ROWS = 256
PAIR = 2 * CHUNK
BAND = PAIR + WIN_CHUNKS * CHUNK
HIST = WIN_CHUNKS * CHUNK
LRU_GROUP = MXU_DIM
N_LRU_GROUPS = LRU_WIDTH // LRU_GROUP
BLOCKS_PER_GROUP = LRU_GROUP // LRU_BLOCK

HEAD_PERM = (0, 4, 1, 5, 2, 6, 3, 7)

BF16 = jnp.bfloat16
F32 = jnp.float32


def _dot(a, b):
    return jnp.dot(a, b, preferred_element_type=F32)


def _dot_nt(a, b):
    return lax.dot_general(a, b, (((1,), (1,)), ((), ())), preferred_element_type=F32)


def _rmsnorm(x, g):
    y = x * lax.rsqrt(jnp.mean(x * x, axis=-1, keepdims=True) + EPS)
    return y * g


def _sigmoid(x):
    return 1.0 / (1.0 + jnp.exp(-x))


def _gelu_tanh(x):
    c = math.sqrt(2.0 / math.pi)
    return 0.5 * x * (1.0 + jnp.tanh(c * (x + 0.044715 * (x * x * x))))


def _resident(shape):
    nd = len(shape)
    return pl.BlockSpec(shape, lambda *_: (0,) * nd, pipeline_mode=pl.Buffered(1))


def _ffn_kernel(x_ref, g_ref, wg_ref, wu_ref, wd_ref, fg_ref, o_ref, *, final_norm):
    x = x_ref[...]
    h = _rmsnorm(x, g_ref[...]).astype(BF16)
    gate = _dot(h, wg_ref[...])
    up = _dot(h, wu_ref[...])
    act = (gate * _sigmoid(gate) * up).astype(BF16)
    y = x + 0.5 * _dot(act, wd_ref[...])
    if final_norm:
        y = _rmsnorm(y, fg_ref[...])
    o_ref[...] = y


def _ffn_call(x2d, g, wg, wu, wd, fg, final_norm):
    t, d = x2d.shape
    f = wg.shape[1]
    rows = pl.BlockSpec((FFN_ROWS, d), lambda i: (i, 0))
    return pl.pallas_call(
        functools.partial(_ffn_kernel, final_norm=final_norm),
        grid=(t // FFN_ROWS,),
        in_specs=[rows, _resident((1, d)), _resident((d, f)), _resident((d, f)),
                  _resident((f, d)), _resident((1, d))],
        out_specs=rows,
        out_shape=jax.ShapeDtypeStruct((t, d), F32),
        compiler_params=pltpu.CompilerParams(
            dimension_semantics=("arbitrary",), vmem_limit_bytes=VMEM_LIMIT_BYTES),
        name="ffn_final" if final_norm else "ffn",
    )(x2d, g, wg, wu, wd, fg)


def _memkv_kernel(mem_ref, g_ref, w_ref, k_ref, v_ref):
    mn = _rmsnorm(mem_ref[0], g_ref[0]).astype(BF16)
    kv = _dot(mn, w_ref[0])
    k_ref[0, 0] = kv[:, :MEM_WIDTH].astype(BF16)
    m = kv.shape[0]
    ones = jnp.ones((m, MEM_HEAD_DIM), BF16)
    for h in range(MEM_HEADS):
        lo = MEM_WIDTH + h * MEM_HEAD_DIM
        v_ref[0, 0, h, :, :MEM_HEAD_DIM] = kv[:, lo:lo + MEM_HEAD_DIM].astype(BF16)
        v_ref[0, 0, h, :, MEM_HEAD_DIM:] = ones


def _memkv_call(mem, mem_norm, w_mem_kv):
    b, m, d = mem.shape
    depth = w_mem_kv.shape[0]
    return pl.pallas_call(
        _memkv_kernel,
        grid=(depth, b),
        in_specs=[pl.BlockSpec((1, m, d), lambda l, i: (i, 0, 0)),
                  pl.BlockSpec((1, 1, d), lambda l, i: (l, 0, 0)),
                  pl.BlockSpec((1, d, 2 * MEM_WIDTH), lambda l, i: (l, 0, 0))],
        out_specs=[pl.BlockSpec((1, 1, m, MEM_WIDTH), lambda l, i: (l, i, 0, 0)),
                   pl.BlockSpec((1, 1, MEM_HEADS, m, 2 * MEM_HEAD_DIM),
                                lambda l, i: (l, i, 0, 0, 0))],
        out_shape=[jax.ShapeDtypeStruct((depth, b, m, MEM_WIDTH), BF16),
                   jax.ShapeDtypeStruct((depth, b, MEM_HEADS, m, 2 * MEM_HEAD_DIM), BF16)],
        compiler_params=pltpu.CompilerParams(
            dimension_semantics=("arbitrary", "arbitrary"), vmem_limit_bytes=VMEM_LIMIT_BYTES),
        name="mem_kv",
    )(mem, mem_norm, w_mem_kv)


def _mix_kernel(x_ref, g_ref, win_ref, gbias_ref, abias_ref, sink_ref, wao_ref,
                cw_ref, cb_ref, wbd_ref, ba_ref, bx_ref, lam_ref, wlo_ref,
                mk_ref, mv_ref, wmo_ref, wout_ref, o_ref,
                kv_scr, xr_scr, hc_scr, h_scr, lru_scr, att_scr):
    j = pl.program_id(1)
    ts = MIX_ROWS

    @pl.when(j == 0)
    def _():
        kv_scr[0:HIST, :] = jnp.zeros((HIST, 3 * LANES), BF16)
        kv_scr[:, 2 * LANES:] = jnp.ones((HIST + ts, LANES), BF16)
        xr_scr[0:SUBLANES, :] = jnp.zeros((SUBLANES, LRU_WIDTH), F32)
        hc_scr[...] = jnp.zeros((1, LRU_WIDTH), F32)

    x = x_ref[0]
    hn = _rmsnorm(x, g_ref[...]).astype(BF16)

    q = _dot(hn, win_ref[:, O_Q:O_Q + ATTN_WIDTH]) * (HEAD_DIM ** -0.5)
    kv = _dot(hn, win_ref[:, O_KV:O_KV + 2 * KV_WIDTH])
    kv_scr[HIST:HIST + ts, 0:2 * LANES] = kv.astype(BF16)

    lane = lax.broadcasted_iota(jnp.int32, (PAIR, LANES), 1)
    low = lane < HEAD_DIM
    col = lax.broadcasted_iota(jnp.int32, (1, BAND), 1)
    neg = jnp.where(j == 0, -jnp.inf, 0.0).astype(F32)
    start_row = jnp.where(col < HIST, neg, 0.0)
    sink = sink_ref[...]

    for p in range(ts // PAIR):
        r0 = p * PAIR
        parts = []
        for c in range(N_Q_HEADS // 2):
            qc = q[r0:r0 + PAIR, c * LANES:(c + 1) * LANES]
            parts.append(jnp.where(low, qc, 0.0))
            parts.append(jnp.where(low, 0.0, qc))
        qs = jnp.concatenate(parts, axis=0).astype(BF16)
        kb = kv_scr[r0:r0 + BAND, 0:LANES]
        vb = kv_scr[r0:r0 + BAND, LANES:3 * LANES]
        s = _dot_nt(qs, kb) + abias_ref[...]
        if p == 0:
            s = s + start_row
        m = jnp.maximum(jnp.max(s, axis=-1, keepdims=True), sink)
        pr = jnp.exp(s - m).astype(BF16)
        o = _dot(pr, vb)
        den = o[:, LANES:] + jnp.exp(sink - m)
        on = o[:, :LANES] / den
        for c in range(N_Q_HEADS // 2):
            a0 = on[(2 * c) * PAIR:(2 * c + 1) * PAIR]
            a1 = on[(2 * c + 1) * PAIR:(2 * c + 2) * PAIR]
            att_scr[r0:r0 + PAIR, c * LANES:(c + 1) * LANES] = jnp.where(low, a0, a1).astype(BF16)

    kv_scr[0:HIST, 0:2 * LANES] = kv_scr[ts:ts + HIST, 0:2 * LANES]
    br_attn = _dot(att_scr[...], wao_ref[...])
    merged = _sigmoid(_dot(hn, win_ref[:, O_G:O_G + D_MODEL]) + gbias_ref[:, 0:D_MODEL]) * br_attn

    row = lax.broadcasted_iota(jnp.int32, (ts, 1), 0)
    sub = row % SUBLANES
    first_token = jnp.logical_and(row == 0, j == 0)
    for gidx in range(N_LRU_GROUPS):
        l0 = gidx * LRU_GROUP
        ls = slice(l0, l0 + LRU_GROUP)
        xr = _dot(hn, win_ref[:, O_XR + l0:O_XR + l0 + LRU_GROUP])
        xr_scr[SUBLANES:SUBLANES + ts, ls] = xr
        xc = cw_ref[3:4, ls] * xr + cb_ref[:, ls]
        for tap in range(CONV_WIDTH - 1):
            back = CONV_WIDTH - 1 - tap
            xc = xc + cw_ref[tap:tap + 1, ls] * xr_scr[SUBLANES - back:SUBLANES - back + ts, ls]
        xr_scr[0:SUBLANES, ls] = xr[ts - SUBLANES:ts]
        ga = _dot(xc.astype(BF16), wbd_ref[gidx])
        r = _sigmoid(ga[:, :LRU_GROUP] + ba_ref[:, ls])
        gi = _sigmoid(ga[:, LRU_GROUP:] + bx_ref[:, ls])
        z = -lam_ref[:, ls]
        softplus = jnp.maximum(z, 0.0) + jnp.log1p(jnp.exp(-jnp.abs(z)))
        a = jnp.exp((-LRU_C * softplus) * r)
        mult = jnp.sqrt(jnp.maximum(1.0 - a * a, 0.0))
        mult = jnp.where(first_token, 1.0, mult)
        b = mult * gi * xc
        for d in (1, 2, 4):
            keep = sub >= d
            a_sh = jnp.where(keep, pltpu.roll(a, d, 0), 1.0)
            b_sh = jnp.where(keep, pltpu.roll(b, d, 0), 0.0)
            b = a * b_sh + b
            a = a * a_sh
        hprev = hc_scr[:, ls]
        for k in range(ts // SUBLANES):
            rs = slice(k * SUBLANES, (k + 1) * SUBLANES)
            hk = a[rs] * hprev + b[rs]
            h_scr[rs, :] = hk
            hprev = hk[SUBLANES - 1:SUBLANES]
        hc_scr[:, ls] = hprev
        yr = _dot(hn, win_ref[:, O_YR + l0:O_YR + l0 + LRU_GROUP])
        lru_scr[:, ls] = (h_scr[...] * _gelu_tanh(yr)).astype(BF16)
    br_lru = _dot(lru_scr[...], wlo_ref[...])
    merged = merged + _sigmoid(
        _dot(hn, win_ref[:, O_G + D_MODEL:O_G + 2 * D_MODEL]) + gbias_ref[:, D_MODEL:2 * D_MODEL]) * br_lru

    cq = _dot(hn, win_ref[:, O_CQ:O_CQ + MEM_WIDTH])
    for h in range(MEM_HEADS):
        hs = slice(h * MEM_HEAD_DIM, (h + 1) * MEM_HEAD_DIM)
        s = _dot_nt(cq[:, hs].astype(BF16), mk_ref[0, 0, :, hs]) * (MEM_HEAD_DIM ** -0.5)
        m = jnp.max(s, axis=-1, keepdims=True)
        pr = jnp.exp(s - m).astype(BF16)
        o = _dot(pr, mv_ref[0, 0, h])
        att_scr[:, hs] = (o[:, :MEM_HEAD_DIM] / o[:, MEM_HEAD_DIM:]).astype(BF16)
    br_mem = _dot(att_scr[...], wmo_ref[...])
    merged = merged + _sigmoid(
        _dot(hn, win_ref[:, O_G + 2 * D_MODEL:O_G + 3 * D_MODEL]) + gbias_ref[:, 2 * D_MODEL:]) * br_mem

    o_ref[0] = x + _dot(merged.astype(BF16), wout_ref[...])


def _mix_call(x, layer, g, win, gbias, abias, sink, wao, cw, cb, wbd, ba, bx, lam, wlo,
              memk, memv, wmo, wout):
    b, s, d = x.shape
    ts = MIX_ROWS
    tile = pl.BlockSpec((1, ts, d), lambda i, j: (i, j, 0))
    m = memk.shape[2]
    in_specs = [
        tile,
        _resident(g.shape), _resident(win.shape), _resident(gbias.shape), _resident(abias.shape),
        _resident(sink.shape), _resident(wao.shape), _resident(cw.shape), _resident(cb.shape),
        _resident(wbd.shape), _resident(ba.shape), _resident(bx.shape), _resident(lam.shape),
        _resident(wlo.shape),
        pl.BlockSpec((1, 1, m, MEM_WIDTH), lambda i, j: (layer, i, 0, 0)),
        pl.BlockSpec((1, 1, MEM_HEADS, m, 2 * MEM_HEAD_DIM), lambda i, j: (layer, i, 0, 0, 0)),
        _resident(wmo.shape), _resident(wout.shape),
    ]
    scratch = [
        pltpu.VMEM((HIST + ts, 3 * LANES), BF16),
        pltpu.VMEM((SUBLANES + ts, LRU_WIDTH), F32),
        pltpu.VMEM((1, LRU_WIDTH), F32),
        pltpu.VMEM((ts, LRU_GROUP), F32),
        pltpu.VMEM((ts, LRU_WIDTH), BF16),
        pltpu.VMEM((ts, ATTN_WIDTH), BF16),
    ]
    return pl.pallas_call(
        _mix_kernel,
        grid=(b, s // ts),
        in_specs=in_specs,
        out_specs=tile,
        out_shape=jax.ShapeDtypeStruct((b, s, d), F32),
        scratch_shapes=scratch,
        compiler_params=pltpu.CompilerParams(
            dimension_semantics=("arbitrary", "arbitrary"), vmem_limit_bytes=VMEM_LIMIT_BYTES),
        name="mixer",
    )(x, g, win, gbias, abias, sink, wao, cw, cb, wbd, ba, bx, lam, wlo, memk, memv, wmo, wout)


def _alibi_bias():
    r = np.arange(PAIR)[:, None]
    c = np.arange(BAND)[None, :]
    dist = np.abs(r + HIST - c).astype(np.float32)
    first = c // CHUNK
    valid = (first >= r // CHUNK) & (first <= r // CHUNK + WIN_CHUNKS)
    out = np.zeros((N_Q_HEADS, PAIR, BAND), np.float32)
    for i, h in enumerate(HEAD_PERM):
        slope = np.float32(2.0 ** (-8.0 * (h + 1) / N_Q_HEADS))
        out[i] = np.where(valid, -(slope * dist), -np.inf)
    return jnp.asarray(out.reshape(N_Q_HEADS * PAIR, BAND))


def _block_diag_groups(w):
    w = w.reshape(N_LRU_GROUPS, BLOCKS_PER_GROUP, LRU_BLOCK, LRU_BLOCK)
    eye = jnp.eye(BLOCKS_PER_GROUP, dtype=w.dtype)
    bd = jnp.einsum("gbij,bc->gbicj", w, eye)
    return bd.reshape(N_LRU_GROUPS, LRU_GROUP, LRU_GROUP)


def kernel(x, mem, ffn1_norm, ffn1_w_gate, ffn1_w_up, ffn1_w_down, mix_norm, w_in, gate_bias,
           attn_sinks, w_attn_out, conv_w, conv_b, lru_wa, lru_ba, lru_wx, lru_bx, lru_lambda,
           w_lru_out, mem_norm, w_mem_kv, w_mem_out, w_out, ffn2_norm, ffn2_w_gate, ffn2_w_up,
           ffn2_w_down, final_norm):
    b, s, d = x.shape
    depth = w_in.shape[0]
    assert d == D_MODEL and w_in.shape[2] == IN_WIDTH
    assert s % MIX_ROWS == 0 and (b * s) % FFN_ROWS == 0 and MIX_ROWS % PAIR == 0

    perm = jnp.asarray(HEAD_PERM)
    abias = _alibi_bias()
    memk, memv = _memkv_call(mem, mem_norm.reshape(depth, 1, d), w_mem_kv.astype(BF16))
    fg = final_norm.reshape(1, d)

    for l in range(depth):
        x = _ffn_call(x.reshape(b * s, d), ffn1_norm[l].reshape(1, d), ffn1_w_gate[l].astype(BF16),
                      ffn1_w_up[l].astype(BF16), ffn1_w_down[l].astype(BF16), fg, False).reshape(b, s, d)

        wq = w_in[l][:, :ATTN_WIDTH].reshape(d, N_Q_HEADS, HEAD_DIM)[:, perm].reshape(d, ATTN_WIDTH)
        win = jnp.concatenate([wq, w_in[l][:, ATTN_WIDTH:]], axis=1).astype(BF16)
        wao = w_attn_out[l].reshape(N_Q_HEADS, HEAD_DIM, d)[perm].reshape(ATTN_WIDTH, d).astype(BF16)
        sink = jnp.repeat(attn_sinks[l][perm], PAIR).reshape(N_Q_HEADS * PAIR, 1)
        wbd = jnp.concatenate([_block_diag_groups(lru_wa[l]), _block_diag_groups(lru_wx[l])],
                              axis=2).astype(BF16)
        x = _mix_call(x, l, mix_norm[l].reshape(1, d), win, gate_bias[l].reshape(1, -1), abias, sink, wao,
                      conv_w[l], conv_b[l].reshape(1, -1), wbd, lru_ba[l].reshape(1, -1),
                      lru_bx[l].reshape(1, -1), lru_lambda[l].reshape(1, -1), w_lru_out[l].astype(BF16),
                      memk, memv, w_mem_out[l].astype(BF16), w_out[l].astype(BF16))

        x = _ffn_call(x.reshape(b * s, d), ffn2_norm[l].reshape(1, d), ffn2_w_gate[l].astype(BF16),
                      ffn2_w_up[l].astype(BF16), ffn2_w_down[l].astype(BF16), fg,
                      l == depth - 1).reshape(b, s, d)
    return x
```

```python
import functools
import math

import numpy as np
import jax
import jax.numpy as jnp
from jax import lax
from jax.experimental import pallas as pl
from jax.experimental.pallas import tpu as pltpu

D_MODEL = 1024
CHUNK = 64
WIN_CHUNKS = 2
HEAD_DIM = 64
N_Q_HEADS = 8
N_KV_HEADS = 2
ATTN_WIDTH = N_Q_HEADS * HEAD_DIM
KV_WIDTH = N_KV_HEADS * HEAD_DIM
LRU_WIDTH = D_MODEL
LRU_BLOCKS = 16
LRU_BLOCK = LRU_WIDTH // LRU_BLOCKS
CONV_WIDTH = 4
LRU_C = 8.0
MEM_HEADS = 4
MEM_WIDTH = D_MODEL // 2
MEM_HEAD_DIM = MEM_WIDTH // MEM_HEADS
N_BRANCH = 3
EPS = 1e-6

O_Q = 0
O_KV = O_Q + ATTN_WIDTH
O_XR = O_KV + 2 * KV_WIDTH
O_YR = O_XR + LRU_WIDTH
O_CQ = O_YR + LRU_WIDTH
O_G = O_CQ + MEM_WIDTH
IN_WIDTH = O_G + N_BRANCH * D_MODEL

LANES = 128
SUBLANES = 8
MXU_DIM = 256
VMEM_LIMIT_BYTES = 56 * 1024 * 1024

FFN_ROWS = 512
MIX_ROWS = 256
BLOCK = 2 * CHUNK
SEG = BLOCK // SUBLANES
BAND = BLOCK + WIN_CHUNKS * CHUNK
HIST = WIN_CHUNKS * CHUNK
LRU_GROUP = MXU_DIM
N_LRU_GROUPS = LRU_WIDTH // LRU_GROUP
BLOCKS_PER_GROUP = LRU_GROUP // LRU_BLOCK
TAPS_BACK = CONV_WIDTH - 1

HEAD_PERM = (0, 4, 1, 5, 2, 6, 3, 7)

BF16 = jnp.bfloat16
F32 = jnp.float32


def _dot(a, b):
    return jnp.dot(a, b, preferred_element_type=F32)


def _dot_nt(a, b):
    return lax.dot_general(a, b, (((1,), (1,)), ((), ())), preferred_element_type=F32)


def _rmsnorm(x, g):
    y = x * lax.rsqrt(jnp.mean(x * x, axis=-1, keepdims=True) + EPS)
    return y * g


def _sigmoid(x):
    return 1.0 / (1.0 + jnp.exp(-x))


def _gelu_tanh(x):
    c = math.sqrt(2.0 / math.pi)
    return 0.5 * x * (1.0 + jnp.tanh(c * (x + 0.044715 * (x * x * x))))


def _resident(shape):
    nd = len(shape)
    return pl.BlockSpec(shape, lambda *_: (0,) * nd, pipeline_mode=pl.Buffered(1))


def _ffn_kernel(x_ref, g_ref, wg_ref, wu_ref, wd_ref, fg_ref, o_ref, *, final_norm):
    x = x_ref[...]
    h = _rmsnorm(x, g_ref[...]).astype(BF16)
    gate = _dot(h, wg_ref[...])
    up = _dot(h, wu_ref[...])
    act = (gate * _sigmoid(gate) * up).astype(BF16)
    y = x + 0.5 * _dot(act, wd_ref[...])
    if final_norm:
        y = _rmsnorm(y, fg_ref[...])
    o_ref[...] = y


def _ffn_call(x2d, g, wg, wu, wd, fg, final_norm):
    t, d = x2d.shape
    f = wg.shape[1]
    rows = pl.BlockSpec((FFN_ROWS, d), lambda i: (i, 0))
    return pl.pallas_call(
        functools.partial(_ffn_kernel, final_norm=final_norm),
        grid=(t // FFN_ROWS,),
        in_specs=[rows, _resident((1, d)), _resident((d, f)), _resident((d, f)),
                  _resident((f, d)), _resident((1, d))],
        out_specs=rows,
        out_shape=jax.ShapeDtypeStruct((t, d), F32),
        compiler_params=pltpu.CompilerParams(
            dimension_semantics=("arbitrary",), vmem_limit_bytes=VMEM_LIMIT_BYTES),
        name="ffn_final" if final_norm else "ffn",
    )(x2d, g, wg, wu, wd, fg)


def _memkv_kernel(mem_ref, g_ref, w_ref, k_ref, v_ref):
    mn = _rmsnorm(mem_ref[0], g_ref[0]).astype(BF16)
    kv = _dot(mn, w_ref[0])
    k_ref[0, 0] = kv[:, :MEM_WIDTH].astype(BF16)
    m = kv.shape[0]
    ones = jnp.ones((m, MEM_HEAD_DIM), BF16)
    for h in range(MEM_HEADS):
        lo = MEM_WIDTH + h * MEM_HEAD_DIM
        v_ref[0, 0, h, :, :MEM_HEAD_DIM] = kv[:, lo:lo + MEM_HEAD_DIM].astype(BF16)
        v_ref[0, 0, h, :, MEM_HEAD_DIM:] = ones


def _memkv_call(mem, mem_norm, w_mem_kv):
    b, m, d = mem.shape
    depth = w_mem_kv.shape[0]
    return pl.pallas_call(
        _memkv_kernel,
        grid=(depth, b),
        in_specs=[pl.BlockSpec((1, m, d), lambda l, i: (i, 0, 0)),
                  pl.BlockSpec((1, 1, d), lambda l, i: (l, 0, 0)),
                  pl.BlockSpec((1, d, 2 * MEM_WIDTH), lambda l, i: (l, 0, 0))],
        out_specs=[pl.BlockSpec((1, 1, m, MEM_WIDTH), lambda l, i: (l, i, 0, 0)),
                   pl.BlockSpec((1, 1, MEM_HEADS, m, 2 * MEM_HEAD_DIM),
                                lambda l, i: (l, i, 0, 0, 0))],
        out_shape=[jax.ShapeDtypeStruct((depth, b, m, MEM_WIDTH), BF16),
                   jax.ShapeDtypeStruct((depth, b, MEM_HEADS, m, 2 * MEM_HEAD_DIM), BF16)],
        compiler_params=pltpu.CompilerParams(
            dimension_semantics=("arbitrary", "arbitrary"), vmem_limit_bytes=VMEM_LIMIT_BYTES),
        name="mem_kv",
    )(mem, mem_norm, w_mem_kv)


def _attn_scores(p, j, q, kv_scr, abias_ref):
    lane = lax.broadcasted_iota(jnp.int32, (BLOCK, LANES), 1)
    low = lane < HEAD_DIM
    r0 = p * BLOCK
    parts = []
    for c in range(N_Q_HEADS // 2):
        qc = q[r0:r0 + BLOCK, c * LANES:(c + 1) * LANES]
        parts.append(jnp.where(low, qc, 0.0))
        parts.append(jnp.where(low, 0.0, qc))
    qs = jnp.concatenate(parts, axis=0).astype(BF16)
    kb = kv_scr[r0:r0 + BAND, 0:LANES]
    s = _dot_nt(qs, kb) + abias_ref[...]
    if p == 0:
        col = lax.broadcasted_iota(jnp.int32, (1, BAND), 1)
        neg = jnp.where(j == 0, -jnp.inf, 0.0).astype(F32)
        s = s + jnp.where(col < HIST, neg, 0.0)
    return s


def _attn_probs(s, sink):
    m = jnp.maximum(jnp.max(s, axis=-1, keepdims=True), sink)
    return jnp.exp(s - m).astype(BF16), jnp.exp(sink - m)


def _attn_values(p, pr, kv_scr):
    r0 = p * BLOCK
    return _dot(pr, kv_scr[r0:r0 + BAND, LANES:3 * LANES])


def _attn_finish(p, o, esink, att_scr):
    lane = lax.broadcasted_iota(jnp.int32, (BLOCK, LANES), 1)
    low = lane < HEAD_DIM
    r0 = p * BLOCK
    on = o[:, :LANES] / (o[:, LANES:] + esink)
    for c in range(N_Q_HEADS // 2):
        a0 = on[(2 * c) * BLOCK:(2 * c + 1) * BLOCK]
        a1 = on[(2 * c + 1) * BLOCK:(2 * c + 2) * BLOCK]
        att_scr[r0:r0 + BLOCK, c * LANES:(c + 1) * LANES] = jnp.where(low, a0, a1).astype(BF16)


def _mem_scores(h, cq, mk_ref):
    hs = slice(h * MEM_HEAD_DIM, (h + 1) * MEM_HEAD_DIM)
    return _dot_nt(cq[:, hs].astype(BF16), mk_ref[0, 0, :, hs]) * (MEM_HEAD_DIM ** -0.5)


def _mem_probs(s):
    return jnp.exp(s - jnp.max(s, axis=-1, keepdims=True)).astype(BF16)


def _mem_finish(h, o, matt_scr):
    hs = slice(h * MEM_HEAD_DIM, (h + 1) * MEM_HEAD_DIM)
    matt_scr[:, hs] = (o[:, :MEM_HEAD_DIM] / o[:, MEM_HEAD_DIM:]).astype(BF16)


def _lru_conv(xr, ls, tail_scr, cw_ref, cb_ref):
    out = []
    for blk in range(MIX_ROWS // BLOCK):
        x = xr[blk * BLOCK:(blk + 1) * BLOCK]
        tails = []
        for m in range(TAPS_BACK):
            k = SEG - TAPS_BACK + m
            base = 2 * SUBLANES * m
            xk = x[k * SUBLANES:(k + 1) * SUBLANES]
            tail_scr[base + SUBLANES:base + 2 * SUBLANES, ls] = xk
            tails.append(tail_scr[base + SUBLANES - 1:base + 2 * SUBLANES - 1, ls])
            tail_scr[base:base + SUBLANES, ls] = xk
        xe = jnp.concatenate(tails + [x], axis=0)
        off = TAPS_BACK * SUBLANES
        xc = cw_ref[TAPS_BACK:CONV_WIDTH, ls] * x + cb_ref[:, ls]
        for tap in range(TAPS_BACK):
            back = TAPS_BACK - tap
            lo = off - back * SUBLANES
            xc = xc + cw_ref[tap:tap + 1, ls] * xe[lo:lo + BLOCK]
        out.append(xc)
    return jnp.concatenate(out, axis=0)


def _lru_scan(a, b, cin):
    rows = [slice(k * SUBLANES, (k + 1) * SUBLANES) for k in range(SEG)]
    hs = [b[rows[0]]]
    ps = [a[rows[0]]]
    for k in range(1, SEG):
        hs.append(a[rows[k]] * hs[-1] + b[rows[k]])
        ps.append(a[rows[k]] * ps[-1])
    sub = lax.broadcasted_iota(jnp.int32, (SUBLANES, 1), 0)
    ap, hp = ps[-1], hs[-1]
    for d in (1, 2, 4):
        keep = sub >= d
        a_sh = jnp.where(keep, pltpu.roll(ap, d, 0), 1.0)
        h_sh = jnp.where(keep, pltpu.roll(hp, d, 0), 0.0)
        hp = ap * h_sh + hp
        ap = ap * a_sh
    keep = sub >= 1
    ax = jnp.where(keep, pltpu.roll(ap, 1, 0), 1.0)
    hx = jnp.where(keep, pltpu.roll(hp, 1, 0), 0.0)
    c = ax * cin + hx
    h = [hs[k] + ps[k] * c for k in range(SEG)]
    return jnp.concatenate(h, axis=0), h[-1][SUBLANES - 1:SUBLANES]


def _lru_pointwise(gidx, j, xc, ga, yr, bah_ref, bxh_ref, lam_ref, hc_scr, lru_scr):
    ts = MIX_ROWS
    ls = slice(gidx * LRU_GROUP, (gidx + 1) * LRU_GROUP)
    t_r = jnp.tanh(ga[:, :LRU_GROUP] + bah_ref[:, ls])
    t_g = jnp.tanh(ga[:, LRU_GROUP:] + bxh_ref[:, ls])
    z = -lam_ref[:, ls]
    softplus = jnp.maximum(z, 0.0) + jnp.log1p(jnp.exp(-jnp.abs(z)))
    ch = (-0.5 * LRU_C) * softplus
    a = jnp.exp(ch + ch * t_r)
    mult = jnp.sqrt(jnp.maximum(1.0 - a * a, 0.0))
    row = lax.broadcasted_iota(jnp.int32, (SUBLANES, 1), 0)
    first_token = jnp.logical_and(row == 0, j == 0)
    mult = jnp.concatenate([jnp.where(first_token, 1.0, mult[:SUBLANES]), mult[SUBLANES:]], axis=0)
    m1 = mult * (0.5 * xc)
    b = m1 + m1 * t_g
    cin = hc_scr[:, ls]
    hparts = []
    for blk in range(ts // BLOCK):
        rs = slice(blk * BLOCK, (blk + 1) * BLOCK)
        hb, cin = _lru_scan(a[rs], b[rs], cin)
        hparts.append(hb)
    hc_scr[:, ls] = cin
    h = jnp.concatenate(hparts, axis=0)
    lru_scr[:, ls] = (h * _gelu_tanh(yr)).astype(BF16)


def _mix_kernel(x_ref, g_ref, win_ref, gbh_ref, abias_ref, sink_ref, wao_ref,
                cw_ref, cb_ref, wbd_ref, bah_ref, bxh_ref, lam_ref, wlo_ref,
                mk_ref, mv_ref, wmo_ref, wout_ref, o_ref,
                kv_scr, tail_scr, hc_scr, lru_scr, att_scr, matt_scr):
    j = pl.program_id(1)
    ts = MIX_ROWS

    @pl.when(j == 0)
    def _():
        kv_scr[0:HIST, :] = jnp.zeros((HIST, 3 * LANES), BF16)
        kv_scr[:, 2 * LANES:] = jnp.ones((HIST + ts, LANES), BF16)
        tail_scr[...] = jnp.zeros(tail_scr.shape, F32)
        hc_scr[...] = jnp.zeros((1, LRU_WIDTH), F32)

    @pl.when(j > 0)
    def _():
        kv_scr[0:HIST, 0:2 * LANES] = kv_scr[ts:ts + HIST, 0:2 * LANES]

    x = x_ref[0]
    hn = _rmsnorm(x, g_ref[...]).astype(BF16)

    sink = sink_ref[...]

    def proj(lo, width):
        return _dot(hn, win_ref[:, lo:lo + width])

    def lru_in(g):
        return proj(O_XR + g * LRU_GROUP, LRU_GROUP), proj(O_YR + g * LRU_GROUP, LRU_GROUP)

    def lru_gates(g, xr):
        ls = slice(g * LRU_GROUP, (g + 1) * LRU_GROUP)
        xc = _lru_conv(xr, ls, tail_scr, cw_ref, cb_ref)
        return xc, _dot(xc.astype(BF16), wbd_ref[g])

    def lru_out(g, xc, ga, yr):
        _lru_pointwise(g, j, xc, ga, yr, bah_ref, bxh_ref, lam_ref, hc_scr, lru_scr)

    def gate_tanh(idx):
        return jnp.tanh(proj(O_G + idx * D_MODEL, D_MODEL) + gbh_ref[:, idx * D_MODEL:(idx + 1) * D_MODEL])

    q = proj(O_Q, ATTN_WIDTH) * (HEAD_DIM ** -0.5)
    kv = proj(O_KV, 2 * KV_WIDTH)
    kv_scr[HIST:HIST + ts, 0:2 * LANES] = kv.astype(BF16)
    xr0, yr0 = lru_in(0)
    s0 = _attn_scores(0, j, q, kv_scr, abias_ref)
    xr1, yr1 = lru_in(1)
    s1 = _attn_scores(1, j, q, kv_scr, abias_ref)
    xc0, ga0 = lru_gates(0, xr0)
    p0, es0 = _attn_probs(s0, sink)
    cq = proj(O_CQ, MEM_WIDTH)
    o0 = _attn_values(0, p0, kv_scr)
    p1, es1 = _attn_probs(s1, sink)
    lru_out(0, xc0, ga0, yr0)
    o1 = _attn_values(1, p1, kv_scr)
    xc1, ga1 = lru_gates(1, xr1)
    ms0 = _mem_scores(0, cq, mk_ref)
    ms1 = _mem_scores(1, cq, mk_ref)
    xr2, yr2 = lru_in(2)
    _attn_finish(0, o0, es0, att_scr)
    _attn_finish(1, o1, es1, att_scr)
    lru_out(1, xc1, ga1, yr1)
    mp0 = _mem_probs(ms0)
    mp1 = _mem_probs(ms1)
    br_attn = _dot(att_scr[...], wao_ref[...])
    xc2, ga2 = lru_gates(2, xr2)
    mo0 = _dot(mp0, mv_ref[0, 0, 0])
    mo1 = _dot(mp1, mv_ref[0, 0, 1])
    t0 = gate_tanh(0)
    ms2 = _mem_scores(2, cq, mk_ref)
    ms3 = _mem_scores(3, cq, mk_ref)
    lru_out(2, xc2, ga2, yr2)
    xr3, yr3 = lru_in(3)
    merged = br_attn + t0 * br_attn
    _mem_finish(0, mo0, matt_scr)
    _mem_finish(1, mo1, matt_scr)
    mp2 = _mem_probs(ms2)
    mp3 = _mem_probs(ms3)
    xc3, ga3 = lru_gates(3, xr3)
    mo2 = _dot(mp2, mv_ref[0, 0, 2])
    mo3 = _dot(mp3, mv_ref[0, 0, 3])
    t2 = gate_tanh(2)
    lru_out(3, xc3, ga3, yr3)
    _mem_finish(2, mo2, matt_scr)
    _mem_finish(3, mo3, matt_scr)
    t1 = gate_tanh(1)
    br_mem = _dot(matt_scr[...], wmo_ref[...])
    merged = merged + (br_mem + t2 * br_mem)
    br_lru = _dot(lru_scr[...], wlo_ref[...])
    merged = merged + (br_lru + t1 * br_lru)

    o_ref[0] = x + _dot(merged.astype(BF16), wout_ref[...])


def _mix_call(x, layer, g, win, gbh, abias, sink, wao, cw, cb, wbd, bah, bxh, lam, wlo,
              memk, memv, wmo, wout):
    b, s, d = x.shape
    ts = MIX_ROWS
    tile = pl.BlockSpec((1, ts, d), lambda i, j: (i, j, 0))
    m = memk.shape[2]
    in_specs = [
        tile,
        _resident(g.shape), _resident(win.shape), _resident(gbh.shape), _resident(abias.shape),
        _resident(sink.shape), _resident(wao.shape), _resident(cw.shape), _resident(cb.shape),
        _resident(wbd.shape), _resident(bah.shape), _resident(bxh.shape), _resident(lam.shape),
        _resident(wlo.shape),
        pl.BlockSpec((1, 1, m, MEM_WIDTH), lambda i, j: (layer, i, 0, 0)),
        pl.BlockSpec((1, 1, MEM_HEADS, m, 2 * MEM_HEAD_DIM), lambda i, j: (layer, i, 0, 0, 0)),
        _resident(wmo.shape), _resident(wout.shape),
    ]
    scratch = [
        pltpu.VMEM((HIST + ts, 3 * LANES), BF16),
        pltpu.VMEM((2 * SUBLANES * TAPS_BACK, LRU_WIDTH), F32),
        pltpu.VMEM((1, LRU_WIDTH), F32),
        pltpu.VMEM((ts, LRU_WIDTH), BF16),
        pltpu.VMEM((ts, ATTN_WIDTH), BF16),
        pltpu.VMEM((ts, MEM_WIDTH), BF16),
    ]
    return pl.pallas_call(
        _mix_kernel,
        grid=(b, s // ts),
        in_specs=in_specs,
        out_specs=tile,
        out_shape=jax.ShapeDtypeStruct((b, s, d), F32),
        scratch_shapes=scratch,
        compiler_params=pltpu.CompilerParams(
            dimension_semantics=("arbitrary", "arbitrary"), vmem_limit_bytes=VMEM_LIMIT_BYTES),
        name="mixer",
    )(x, g, win, gbh, abias, sink, wao, cw, cb, wbd, bah, bxh, lam, wlo, memk, memv, wmo, wout)


def _row_token():
    r = np.arange(BLOCK)
    return SEG * (r % SUBLANES) + r // SUBLANES


def _alibi_bias():
    tq = _row_token()[:, None]
    c = np.arange(BAND)
    tk = (BLOCK * (c // BLOCK) + _row_token()[c % BLOCK])[None, :]
    dist = np.abs(tq + HIST - tk).astype(np.float32)
    kc = tk // CHUNK
    valid = (kc >= tq // CHUNK) & (kc <= tq // CHUNK + WIN_CHUNKS)
    out = np.zeros((N_Q_HEADS, BLOCK, BAND), np.float32)
    for i, h in enumerate(HEAD_PERM):
        slope = np.float32(2.0 ** (-8.0 * (h + 1) / N_Q_HEADS))
        out[i] = np.where(valid, -(slope * dist), -np.inf)
    return jnp.asarray(out.reshape(N_Q_HEADS * BLOCK, BAND))


def _block_diag_groups(w):
    w = w.reshape(N_LRU_GROUPS, BLOCKS_PER_GROUP, LRU_BLOCK, LRU_BLOCK)
    eye = jnp.eye(BLOCKS_PER_GROUP, dtype=w.dtype)
    bd = jnp.einsum("gbij,bc->gbicj", w, eye)
    return bd.reshape(N_LRU_GROUPS, LRU_GROUP, LRU_GROUP)


def _interleave(x, inverse=False):
    b, s, d = x.shape
    shape = (b, s // BLOCK, SEG, SUBLANES, d) if inverse else (b, s // BLOCK, SUBLANES, SEG, d)
    return x.reshape(shape).transpose(0, 1, 3, 2, 4).reshape(b, s, d)


def kernel(x, mem, ffn1_norm, ffn1_w_gate, ffn1_w_up, ffn1_w_down, mix_norm, w_in, gate_bias,
           attn_sinks, w_attn_out, conv_w, conv_b, lru_wa, lru_ba, lru_wx, lru_bx, lru_lambda,
           w_lru_out, mem_norm, w_mem_kv, w_mem_out, w_out, ffn2_norm, ffn2_w_gate, ffn2_w_up,
           ffn2_w_down, final_norm):
    b, s, d = x.shape
    depth = w_in.shape[0]
    assert d == D_MODEL and w_in.shape[2] == IN_WIDTH
    assert s % MIX_ROWS == 0 and (b * s) % FFN_ROWS == 0 and MIX_ROWS % BLOCK == 0

    perm = jnp.asarray(HEAD_PERM)
    abias = _alibi_bias()
    memk, memv = _memkv_call(mem, mem_norm.reshape(depth, 1, d), w_mem_kv.astype(BF16))
    fg = final_norm.reshape(1, d)
    x = _interleave(x)

    for l in range(depth):
        x = _ffn_call(x.reshape(b * s, d), ffn1_norm[l].reshape(1, d), ffn1_w_gate[l].astype(BF16),
                      ffn1_w_up[l].astype(BF16), ffn1_w_down[l].astype(BF16), fg, False).reshape(b, s, d)

        wq = w_in[l][:, :ATTN_WIDTH].reshape(d, N_Q_HEADS, HEAD_DIM)[:, perm].reshape(d, ATTN_WIDTH)
        win = jnp.concatenate([wq, w_in[l][:, ATTN_WIDTH:O_G], 0.5 * w_in[l][:, O_G:]], axis=1).astype(BF16)
        wao = w_attn_out[l].reshape(N_Q_HEADS, HEAD_DIM, d)[perm].reshape(ATTN_WIDTH, d).astype(BF16)
        sink = jnp.repeat(attn_sinks[l][perm], BLOCK).reshape(N_Q_HEADS * BLOCK, 1)
        wbd = (0.5 * jnp.concatenate([_block_diag_groups(lru_wa[l]), _block_diag_groups(lru_wx[l])],
                                     axis=2)).astype(BF16)
        x = _mix_call(x, l, mix_norm[l].reshape(1, d), win, 0.5 * gate_bias[l].reshape(1, -1), abias, sink,
                      wao, conv_w[l], conv_b[l].reshape(1, -1), wbd, 0.5 * lru_ba[l].reshape(1, -1),
                      0.5 * lru_bx[l].reshape(1, -1), lru_lambda[l].reshape(1, -1),
                      w_lru_out[l].astype(BF16), memk, memv, w_mem_out[l].astype(BF16),
                      (0.5 * w_out[l]).astype(BF16))

        x = _ffn_call(x.reshape(b * s, d), ffn2_norm[l].reshape(1, d), ffn2_w_gate[l].astype(BF16),
                      ffn2_w_up[l].astype(BF16), ffn2_w_down[l].astype(BF16), fg,
                      l == depth - 1).reshape(b, s, d)
    return _interleave(x, inverse=True)
```

```python
import functools
import math

import numpy as np
import jax
import jax.numpy as jnp
from jax import lax
from jax.experimental import pallas as pl
from jax.experimental.pallas import tpu as pltpu

D_MODEL = 1024
CHUNK = 64
WIN_CHUNKS = 2
HEAD_DIM = 64
N_Q_HEADS = 8
N_KV_HEADS = 2
ATTN_WIDTH = N_Q_HEADS * HEAD_DIM
KV_WIDTH = N_KV_HEADS * HEAD_DIM
LRU_WIDTH = D_MODEL
LRU_BLOCKS = 16
LRU_BLOCK = LRU_WIDTH // LRU_BLOCKS
CONV_WIDTH = 4
LRU_C = 8.0
MEM_HEADS = 4
MEM_WIDTH = D_MODEL // 2
MEM_HEAD_DIM = MEM_WIDTH // MEM_HEADS
N_BRANCH = 3
EPS = 1e-6

O_Q = 0
O_KV = O_Q + ATTN_WIDTH
O_XR = O_KV + 2 * KV_WIDTH
O_YR = O_XR + LRU_WIDTH
O_CQ = O_YR + LRU_WIDTH
O_G = O_CQ + MEM_WIDTH
IN_WIDTH = O_G + N_BRANCH * D_MODEL

LANES = 128
SUBLANES = 8
MXU_DIM = 256
VMEM_LIMIT_BYTES = 56 * 1024 * 1024

FFN_ROWS = 1024
FFN_SUB_ROWS = 512
MIX_ROWS = 256
MIX_SUB_ROWS = 256
BLOCK = 2 * CHUNK
SEG = BLOCK // SUBLANES
BAND = BLOCK + WIN_CHUNKS * CHUNK
HIST = WIN_CHUNKS * CHUNK
LRU_GROUP = MXU_DIM
N_LRU_GROUPS = LRU_WIDTH // LRU_GROUP
BLOCKS_PER_GROUP = LRU_GROUP // LRU_BLOCK
TAPS_BACK = CONV_WIDTH - 1

HEAD_PERM = (0, 4, 1, 5, 2, 6, 3, 7)

BF16 = jnp.bfloat16
F32 = jnp.float32


def _dot(a, b):
    return jnp.dot(a, b, preferred_element_type=F32)


def _dot_nt(a, b):
    return lax.dot_general(a, b, (((1,), (1,)), ((), ())), preferred_element_type=F32)


def _rmsnorm(x, g):
    y = x * lax.rsqrt(jnp.mean(x * x, axis=-1, keepdims=True) + EPS)
    return y * g


def _sigmoid(x):
    return 1.0 / (1.0 + jnp.exp(-x))


def _gelu_tanh(x):
    c = math.sqrt(2.0 / math.pi)
    return 0.5 * x * (1.0 + jnp.tanh(c * (x + 0.044715 * (x * x * x))))


def _resident(shape):
    nd = len(shape)
    return pl.BlockSpec(shape, lambda *_: (0,) * nd, pipeline_mode=pl.Buffered(1))


def _layer_resident(stacked, layer):
    nd = stacked.ndim - 1
    return pl.BlockSpec((None,) + stacked.shape[1:], lambda *_: (layer,) + (0,) * nd,
                        pipeline_mode=pl.Buffered(1))


def _ffn_kernel(x_ref, g_ref, wg_ref, wu_ref, wd_ref, fg_ref, o_ref, *, final_norm):
    for r0 in range(0, FFN_ROWS, FFN_SUB_ROWS):
        rs = slice(r0, r0 + FFN_SUB_ROWS)
        x = x_ref[rs, :]
        h = _rmsnorm(x, g_ref[...]).astype(BF16)
        gate = _dot(h, wg_ref[...])
        up = _dot(h, wu_ref[...])
        act = (gate * _sigmoid(gate) * up).astype(BF16)
        y = x + 0.5 * _dot(act, wd_ref[...])
        if final_norm:
            y = _rmsnorm(y, fg_ref[...])
        o_ref[rs, :] = y


def _ffn_call(x2d, layer, g, wg, wu, wd, fg, final_norm):
    t, d = x2d.shape
    rows = pl.BlockSpec((FFN_ROWS, d), lambda i: (i, 0))
    return pl.pallas_call(
        functools.partial(_ffn_kernel, final_norm=final_norm),
        grid=(t // FFN_ROWS,),
        in_specs=[rows, _layer_resident(g, layer), _layer_resident(wg, layer), _layer_resident(wu, layer),
                  _layer_resident(wd, layer), _resident((1, d))],
        out_specs=rows,
        out_shape=jax.ShapeDtypeStruct((t, d), F32),
        compiler_params=pltpu.CompilerParams(
            dimension_semantics=("arbitrary",), vmem_limit_bytes=VMEM_LIMIT_BYTES),
        name="ffn_final" if final_norm else "ffn",
    )(x2d, g, wg, wu, wd, fg)


def _memkv_kernel(mem_ref, g_ref, w_ref, k_ref, v_ref):
    mn = _rmsnorm(mem_ref[0], g_ref[0]).astype(BF16)
    kv = _dot(mn, w_ref[0])
    k_ref[0, 0] = kv[:, :MEM_WIDTH].astype(BF16)
    m = kv.shape[0]
    ones = jnp.ones((m, MEM_HEAD_DIM), BF16)
    for h in range(MEM_HEADS):
        lo = MEM_WIDTH + h * MEM_HEAD_DIM
        v_ref[0, 0, h, :, :MEM_HEAD_DIM] = kv[:, lo:lo + MEM_HEAD_DIM].astype(BF16)
        v_ref[0, 0, h, :, MEM_HEAD_DIM:] = ones


def _memkv_call(mem, mem_norm, w_mem_kv):
    b, m, d = mem.shape
    depth = w_mem_kv.shape[0]
    return pl.pallas_call(
        _memkv_kernel,
        grid=(depth, b),
        in_specs=[pl.BlockSpec((1, m, d), lambda l, i: (i, 0, 0)),
                  pl.BlockSpec((1, 1, d), lambda l, i: (l, 0, 0)),
                  pl.BlockSpec((1, d, 2 * MEM_WIDTH), lambda l, i: (l, 0, 0))],
        out_specs=[pl.BlockSpec((1, 1, m, MEM_WIDTH), lambda l, i: (l, i, 0, 0)),
                   pl.BlockSpec((1, 1, MEM_HEADS, m, 2 * MEM_HEAD_DIM),
                                lambda l, i: (l, i, 0, 0, 0))],
        out_shape=[jax.ShapeDtypeStruct((depth, b, m, MEM_WIDTH), BF16),
                   jax.ShapeDtypeStruct((depth, b, MEM_HEADS, m, 2 * MEM_HEAD_DIM), BF16)],
        compiler_params=pltpu.CompilerParams(
            dimension_semantics=("arbitrary", "arbitrary"), vmem_limit_bytes=VMEM_LIMIT_BYTES),
        name="mem_kv",
    )(mem, mem_norm, w_mem_kv)


def _attn_scores(p, first, j, q, kv_scr, abias_ref):
    lane = lax.broadcasted_iota(jnp.int32, (BLOCK, LANES), 1)
    low = lane < HEAD_DIM
    r0 = p * BLOCK
    parts = []
    for c in range(N_Q_HEADS // 2):
        qc = q[r0:r0 + BLOCK, c * LANES:(c + 1) * LANES]
        parts.append(jnp.where(low, qc, 0.0))
        parts.append(jnp.where(low, 0.0, qc))
    qs = jnp.concatenate(parts, axis=0).astype(BF16)
    kb = kv_scr[r0:r0 + BAND, 0:LANES]
    s = _dot_nt(qs, kb) + abias_ref[...]
    if first:
        col = lax.broadcasted_iota(jnp.int32, (1, BAND), 1)
        neg = jnp.where(j == 0, -jnp.inf, 0.0).astype(F32)
        s = s + jnp.where(col < HIST, neg, 0.0)
    return s


def _attn_probs(s, sink):
    m = jnp.maximum(jnp.max(s, axis=-1, keepdims=True), sink)
    return jnp.exp(s - m).astype(BF16), jnp.exp(sink - m)


def _attn_values(p, pr, kv_scr):
    r0 = p * BLOCK
    return _dot(pr, kv_scr[r0:r0 + BAND, LANES:3 * LANES])


def _attn_finish(p, o, esink, att_scr):
    lane = lax.broadcasted_iota(jnp.int32, (BLOCK, LANES), 1)
    low = lane < HEAD_DIM
    r0 = p * BLOCK
    on = o[:, :LANES] / (o[:, LANES:] + esink)
    for c in range(N_Q_HEADS // 2):
        a0 = on[(2 * c) * BLOCK:(2 * c + 1) * BLOCK]
        a1 = on[(2 * c + 1) * BLOCK:(2 * c + 2) * BLOCK]
        att_scr[r0:r0 + BLOCK, c * LANES:(c + 1) * LANES] = jnp.where(low, a0, a1).astype(BF16)


def _mem_scores(h, cq, mk_ref):
    hs = slice(h * MEM_HEAD_DIM, (h + 1) * MEM_HEAD_DIM)
    return _dot_nt(cq[:, hs].astype(BF16), mk_ref[0, 0, :, hs]) * (MEM_HEAD_DIM ** -0.5)


def _mem_probs(s):
    return jnp.exp(s - jnp.max(s, axis=-1, keepdims=True)).astype(BF16)


def _mem_finish(h, o, matt_scr):
    hs = slice(h * MEM_HEAD_DIM, (h + 1) * MEM_HEAD_DIM)
    matt_scr[:, hs] = (o[:, :MEM_HEAD_DIM] / o[:, MEM_HEAD_DIM:]).astype(BF16)


def _lru_conv(xr, ls, tail_scr, cw_ref, cb_ref):
    out = []
    for blk in range(xr.shape[0] // BLOCK):
        x = xr[blk * BLOCK:(blk + 1) * BLOCK]
        tails = []
        for m in range(TAPS_BACK):
            k = SEG - TAPS_BACK + m
            base = 2 * SUBLANES * m
            xk = x[k * SUBLANES:(k + 1) * SUBLANES]
            tail_scr[base + SUBLANES:base + 2 * SUBLANES, ls] = xk
            tails.append(tail_scr[base + SUBLANES - 1:base + 2 * SUBLANES - 1, ls])
            tail_scr[base:base + SUBLANES, ls] = xk
        xe = jnp.concatenate(tails + [x], axis=0)
        off = TAPS_BACK * SUBLANES
        xc = cw_ref[TAPS_BACK:CONV_WIDTH, ls] * x + cb_ref[:, ls]
        for tap in range(TAPS_BACK):
            back = TAPS_BACK - tap
            lo = off - back * SUBLANES
            xc = xc + cw_ref[tap:tap + 1, ls] * xe[lo:lo + BLOCK]
        out.append(xc)
    return jnp.concatenate(out, axis=0)


def _lru_scan(a, b, cin):
    rows = [slice(k * SUBLANES, (k + 1) * SUBLANES) for k in range(SEG)]
    hs = [b[rows[0]]]
    ps = [a[rows[0]]]
    for k in range(1, SEG):
        hs.append(a[rows[k]] * hs[-1] + b[rows[k]])
        ps.append(a[rows[k]] * ps[-1])
    sub = lax.broadcasted_iota(jnp.int32, (SUBLANES, 1), 0)
    ap, hp = ps[-1], hs[-1]
    for d in (1, 2, 4):
        keep = sub >= d
        a_sh = jnp.where(keep, pltpu.roll(ap, d, 0), 1.0)
        h_sh = jnp.where(keep, pltpu.roll(hp, d, 0), 0.0)
        hp = ap * h_sh + hp
        ap = ap * a_sh
    keep = sub >= 1
    ax = jnp.where(keep, pltpu.roll(ap, 1, 0), 1.0)
    hx = jnp.where(keep, pltpu.roll(hp, 1, 0), 0.0)
    c = ax * cin + hx
    h = [hs[k] + ps[k] * c for k in range(SEG)]
    return jnp.concatenate(h, axis=0), h[-1][SUBLANES - 1:SUBLANES]


def _lru_pointwise(gidx, blk, first, j, xc, ga, yr, bah_ref, bxh_ref, lam_ref, hc_scr, lru_scr):
    ls = slice(gidx * LRU_GROUP, (gidx + 1) * LRU_GROUP)
    rs = slice(blk * BLOCK, (blk + 1) * BLOCK)
    t_r = jnp.tanh(ga[rs, :LRU_GROUP] + bah_ref[:, ls])
    t_g = jnp.tanh(ga[rs, LRU_GROUP:] + bxh_ref[:, ls])
    z = -lam_ref[:, ls]
    softplus = jnp.maximum(z, 0.0) + jnp.log1p(jnp.exp(-jnp.abs(z)))
    ch = (-0.5 * LRU_C) * softplus
    a = jnp.exp(ch + ch * t_r)
    mult = jnp.sqrt(jnp.maximum(1.0 - a * a, 0.0))
    if first:
        row = lax.broadcasted_iota(jnp.int32, (SUBLANES, 1), 0)
        first_token = jnp.logical_and(row == 0, j == 0)
        mult = jnp.concatenate([jnp.where(first_token, 1.0, mult[:SUBLANES]), mult[SUBLANES:]], axis=0)
    m1 = mult * (0.5 * xc[rs])
    b = m1 + m1 * t_g
    h, cout = _lru_scan(a, b, hc_scr[:, ls])
    hc_scr[:, ls] = cout
    lru_scr[rs, ls] = (h * _gelu_tanh(yr[rs])).astype(BF16)


def _mix_kernel(x_ref, g_ref, win_ref, gbh_ref, abias_ref, sink_ref, wao_ref,
                cw_ref, cb_ref, wbd_ref, bah_ref, bxh_ref, lam_ref, wlo_ref,
                mk_ref, mv_ref, wmo_ref, wout_ref, o_ref,
                kv_scr, tail_scr, hc_scr, lru_scr, att_scr, matt_scr):
    j = pl.program_id(1)
    ts = MIX_ROWS

    @pl.when(j == 0)
    def _():
        kv_scr[0:HIST, :] = jnp.zeros((HIST, 3 * LANES), BF16)
        kv_scr[:, 2 * LANES:] = jnp.ones((HIST + ts, LANES), BF16)
        tail_scr[...] = jnp.zeros(tail_scr.shape, F32)
        hc_scr[...] = jnp.zeros((1, LRU_WIDTH), F32)

    @pl.when(j > 0)
    def _():
        kv_scr[0:HIST, 0:2 * LANES] = kv_scr[ts:ts + HIST, 0:2 * LANES]

    for u in range(ts // MIX_SUB_ROWS):
        rows = pl.ds(u * MIX_SUB_ROWS, MIX_SUB_ROWS)
        _mix_sub_tile(u == 0, j, x_ref.at[0].at[rows], g_ref, win_ref, gbh_ref, abias_ref, sink_ref, wao_ref,
                      cw_ref, cb_ref, wbd_ref, bah_ref, bxh_ref, lam_ref, wlo_ref,
                      mk_ref, mv_ref, wmo_ref, wout_ref, o_ref.at[0].at[rows],
                      kv_scr.at[pl.ds(u * MIX_SUB_ROWS, HIST + MIX_SUB_ROWS)], tail_scr, hc_scr,
                      lru_scr.at[rows], att_scr.at[rows], matt_scr.at[rows])


def _mix_sub_tile(head, j, x_ref, g_ref, win_ref, gbh_ref, abias_ref, sink_ref, wao_ref,
                  cw_ref, cb_ref, wbd_ref, bah_ref, bxh_ref, lam_ref, wlo_ref,
                  mk_ref, mv_ref, wmo_ref, wout_ref, o_ref,
                  kv_scr, tail_scr, hc_scr, lru_scr, att_scr, matt_scr):
    ts = MIX_SUB_ROWS
    x = x_ref[...]
    hn = _rmsnorm(x, g_ref[...]).astype(BF16)

    sink = sink_ref[...]

    def proj(lo, width):
        return _dot(hn, win_ref[:, lo:lo + width])

    def pxr(g):
        return proj(O_XR + g * LRU_GROUP, LRU_GROUP)

    def pyr(g):
        return proj(O_YR + g * LRU_GROUP, LRU_GROUP)

    def lru_gates(g, xr):
        ls = slice(g * LRU_GROUP, (g + 1) * LRU_GROUP)
        xc = _lru_conv(xr, ls, tail_scr, cw_ref, cb_ref)
        return xc, _dot(xc.astype(BF16), wbd_ref[g])

    def lru_out(g, blk, xc, ga, yr):
        _lru_pointwise(g, blk, head and blk == 0, j, xc, ga, yr, bah_ref, bxh_ref, lam_ref, hc_scr, lru_scr)

    def gate_part(idx, c):
        return proj(O_G + idx * D_MODEL + c * MXU_DIM, MXU_DIM)

    def gated(idx, parts, br):
        t = jnp.tanh(jnp.concatenate(parts, axis=1) + gbh_ref[:, idx * D_MODEL:(idx + 1) * D_MODEL])
        return br + t * br

    att = lambda p: _attn_scores(p, head and p == 0, j, q, kv_scr, abias_ref)

    q = proj(O_Q, ATTN_WIDTH) * (HEAD_DIM ** -0.5)
    kv = proj(O_KV, 2 * KV_WIDTH)
    kv_scr[HIST:HIST + ts, 0:2 * LANES] = kv.astype(BF16)
    n_parts = D_MODEL // MXU_DIM
    xr0, yr0 = pxr(0), pyr(0)
    s0 = att(0)
    xr1, yr1 = pxr(1), pyr(1)
    s1 = att(1)
    xc0, ga0 = lru_gates(0, xr0)
    p0, es0 = _attn_probs(s0, sink)
    cq = proj(O_CQ, MEM_WIDTH)
    o0 = _attn_values(0, p0, kv_scr)
    p1, es1 = _attn_probs(s1, sink)
    lru_out(0, 0, xc0, ga0, yr0)
    lru_out(0, 1, xc0, ga0, yr0)
    o1 = _attn_values(1, p1, kv_scr)
    xc1, ga1 = lru_gates(1, xr1)
    ms0 = _mem_scores(0, cq, mk_ref)
    ms1 = _mem_scores(1, cq, mk_ref)
    xr2, yr2 = pxr(2), pyr(2)
    _attn_finish(0, o0, es0, att_scr)
    _attn_finish(1, o1, es1, att_scr)
    lru_out(1, 0, xc1, ga1, yr1)
    lru_out(1, 1, xc1, ga1, yr1)
    mp0 = _mem_probs(ms0)
    mp1 = _mem_probs(ms1)
    br_attn = _dot(att_scr[...], wao_ref[...])
    xc2, ga2 = lru_gates(2, xr2)
    mo0 = _dot(mp0, mv_ref[0, 0, 0])
    mo1 = _dot(mp1, mv_ref[0, 0, 1])
    g0 = [gate_part(0, c) for c in range(n_parts)]
    ms2 = _mem_scores(2, cq, mk_ref)
    ms3 = _mem_scores(3, cq, mk_ref)
    lru_out(2, 0, xc2, ga2, yr2)
    lru_out(2, 1, xc2, ga2, yr2)
    xr3, yr3 = pxr(3), pyr(3)
    merged = gated(0, g0, br_attn)
    _mem_finish(0, mo0, matt_scr)
    _mem_finish(1, mo1, matt_scr)
    mp2 = _mem_probs(ms2)
    mp3 = _mem_probs(ms3)
    xc3, ga3 = lru_gates(3, xr3)
    mo2 = _dot(mp2, mv_ref[0, 0, 2])
    mo3 = _dot(mp3, mv_ref[0, 0, 3])
    g2 = [gate_part(2, c) for c in range(n_parts)]
    lru_out(3, 0, xc3, ga3, yr3)
    lru_out(3, 1, xc3, ga3, yr3)
    _mem_finish(2, mo2, matt_scr)
    _mem_finish(3, mo3, matt_scr)
    g1 = [gate_part(1, c) for c in range(n_parts)]
    br_mem = _dot(matt_scr[...], wmo_ref[...])
    merged = merged + gated(2, g2, br_mem)
    br_lru = _dot(lru_scr[...], wlo_ref[...])
    merged = merged + gated(1, g1, br_lru)

    o_ref[...] = x + _dot(merged.astype(BF16), wout_ref[...])


def _mix_call(x, layer, g, win, gbh, abias, sink, wao, cw, cb, wbd, bah, bxh, lam, wlo,
              memk, memv, wmo, wout):
    b, s, d = x.shape
    ts = MIX_ROWS
    tile = pl.BlockSpec((1, ts, d), lambda i, j: (i, j, 0))
    m = memk.shape[2]
    per_layer = functools.partial(_layer_resident, layer=layer)
    in_specs = [
        tile,
        per_layer(g), per_layer(win), per_layer(gbh), _resident(abias.shape),
        per_layer(sink), per_layer(wao), per_layer(cw), per_layer(cb),
        per_layer(wbd), per_layer(bah), per_layer(bxh), per_layer(lam),
        per_layer(wlo),
        pl.BlockSpec((1, 1, m, MEM_WIDTH), lambda i, j: (layer, i, 0, 0)),
        pl.BlockSpec((1, 1, MEM_HEADS, m, 2 * MEM_HEAD_DIM), lambda i, j: (layer, i, 0, 0, 0)),
        per_layer(wmo), per_layer(wout),
    ]
    scratch = [
        pltpu.VMEM((HIST + ts, 3 * LANES), BF16),
        pltpu.VMEM((2 * SUBLANES * TAPS_BACK, LRU_WIDTH), F32),
        pltpu.VMEM((1, LRU_WIDTH), F32),
        pltpu.VMEM((ts, LRU_WIDTH), BF16),
        pltpu.VMEM((ts, ATTN_WIDTH), BF16),
        pltpu.VMEM((ts, MEM_WIDTH), BF16),
    ]
    return pl.pallas_call(
        _mix_kernel,
        grid=(b, s // ts),
        in_specs=in_specs,
        out_specs=tile,
        out_shape=jax.ShapeDtypeStruct((b, s, d), F32),
        scratch_shapes=scratch,
        compiler_params=pltpu.CompilerParams(
            dimension_semantics=("arbitrary", "arbitrary"), vmem_limit_bytes=VMEM_LIMIT_BYTES),
        name="mixer",
    )(x, g, win, gbh, abias, sink, wao, cw, cb, wbd, bah, bxh, lam, wlo, memk, memv, wmo, wout)


def _row_token():
    r = np.arange(BLOCK)
    return SEG * (r % SUBLANES) + r // SUBLANES


def _alibi_bias():
    tq = _row_token()[:, None]
    c = np.arange(BAND)
    tk = (BLOCK * (c // BLOCK) + _row_token()[c % BLOCK])[None, :]
    dist = np.abs(tq + HIST - tk).astype(np.float32)
    kc = tk // CHUNK
    valid = (kc >= tq // CHUNK) & (kc <= tq // CHUNK + WIN_CHUNKS)
    out = np.zeros((N_Q_HEADS, BLOCK, BAND), np.float32)
    for i, h in enumerate(HEAD_PERM):
        slope = np.float32(2.0 ** (-8.0 * (h + 1) / N_Q_HEADS))
        out[i] = np.where(valid, -(slope * dist), -np.inf)
    return jnp.asarray(out.reshape(N_Q_HEADS * BLOCK, BAND))


def _block_diag_groups(w):
    depth = w.shape[0]
    w = w.reshape(depth, N_LRU_GROUPS, BLOCKS_PER_GROUP, LRU_BLOCK, LRU_BLOCK)
    eye = jnp.eye(BLOCKS_PER_GROUP, dtype=w.dtype)
    bd = jnp.einsum("lgbij,bc->lgbicj", w, eye)
    return bd.reshape(depth, N_LRU_GROUPS, LRU_GROUP, LRU_GROUP)


def _interleave(x, inverse=False):
    b, s, d = x.shape
    shape = (b, s // BLOCK, SEG, SUBLANES, d) if inverse else (b, s // BLOCK, SUBLANES, SEG, d)
    return x.reshape(shape).transpose(0, 1, 3, 2, 4).reshape(b, s, d)


def kernel(x, mem, ffn1_norm, ffn1_w_gate, ffn1_w_up, ffn1_w_down, mix_norm, w_in, gate_bias,
           attn_sinks, w_attn_out, conv_w, conv_b, lru_wa, lru_ba, lru_wx, lru_bx, lru_lambda,
           w_lru_out, mem_norm, w_mem_kv, w_mem_out, w_out, ffn2_norm, ffn2_w_gate, ffn2_w_up,
           ffn2_w_down, final_norm):
    b, s, d = x.shape
    depth = w_in.shape[0]
    assert d == D_MODEL and w_in.shape[2] == IN_WIDTH
    assert s % MIX_ROWS == 0 and (b * s) % FFN_ROWS == 0
    assert MIX_ROWS % MIX_SUB_ROWS == 0 and MIX_SUB_ROWS % BLOCK == 0 and FFN_ROWS % FFN_SUB_ROWS == 0

    def row(v):
        return v.reshape(depth, 1, -1)

    perm = jnp.asarray(HEAD_PERM)
    wq = w_in[:, :, :ATTN_WIDTH].reshape(depth, d, N_Q_HEADS, HEAD_DIM)[:, :, perm].reshape(depth, d, ATTN_WIDTH)
    win = jnp.concatenate([wq, w_in[:, :, ATTN_WIDTH:O_G], 0.5 * w_in[:, :, O_G:]], axis=2).astype(BF16)
    wao = w_attn_out.reshape(depth, N_Q_HEADS, HEAD_DIM, d)[:, perm].reshape(depth, ATTN_WIDTH, d).astype(BF16)
    sink = jnp.repeat(attn_sinks[:, perm], BLOCK, axis=1).reshape(depth, N_Q_HEADS * BLOCK, 1)
    wbd = (0.5 * jnp.concatenate([_block_diag_groups(lru_wa), _block_diag_groups(lru_wx)], axis=3)).astype(BF16)
    mix_params = (row(mix_norm), win, row(0.5 * gate_bias), _alibi_bias(), sink, wao, conv_w, row(conv_b), wbd,
                  row(0.5 * lru_ba), row(0.5 * lru_bx), row(lru_lambda), w_lru_out.astype(BF16))
    wmo, wout = w_mem_out.astype(BF16), (0.5 * w_out).astype(BF16)
    ffn1 = (row(ffn1_norm), ffn1_w_gate.astype(BF16), ffn1_w_up.astype(BF16), ffn1_w_down.astype(BF16))
    ffn2 = (row(ffn2_norm), ffn2_w_gate.astype(BF16), ffn2_w_up.astype(BF16), ffn2_w_down.astype(BF16))
    fg = final_norm.reshape(1, d)

    memk, memv = _memkv_call(mem, row(mem_norm), w_mem_kv.astype(BF16))
    x = _interleave(x)
    for l in range(depth):
        x = _ffn_call(x.reshape(b * s, d), l, *ffn1, fg, False).reshape(b, s, d)
        x = _mix_call(x, l, *mix_params, memk, memv, wmo, wout)
        x = _ffn_call(x.reshape(b * s, d), l, *ffn2, fg, l == depth - 1).reshape(b, s, d)
    return _interleave(x, inverse=True)
```

```python
import functools
import math

import numpy as np
import jax
import jax.numpy as jnp
from jax import lax
from jax.experimental import pallas as pl
from jax.experimental.pallas import tpu as pltpu

D_MODEL = 1024
CHUNK = 64
WIN_CHUNKS = 2
HEAD_DIM = 64
N_Q_HEADS = 8
N_KV_HEADS = 2
ATTN_WIDTH = N_Q_HEADS * HEAD_DIM
KV_WIDTH = N_KV_HEADS * HEAD_DIM
LRU_WIDTH = D_MODEL
LRU_BLOCKS = 16
LRU_BLOCK = LRU_WIDTH // LRU_BLOCKS
CONV_WIDTH = 4
LRU_C = 8.0
MEM_HEADS = 4
MEM_WIDTH = D_MODEL // 2
MEM_HEAD_DIM = MEM_WIDTH // MEM_HEADS
N_BRANCH = 3
EPS = 1e-6
LOG2E = math.log2(math.e)

O_Q = 0
O_KV = O_Q + ATTN_WIDTH
O_XR = O_KV + 2 * KV_WIDTH
O_YR = O_XR + LRU_WIDTH
O_CQ = O_YR + LRU_WIDTH
O_G = O_CQ + MEM_WIDTH
IN_WIDTH = O_G + N_BRANCH * D_MODEL

LANES = 128
SUBLANES = 8
MXU_DIM = 256
VMEM_LIMIT_BYTES = 56 * 1024 * 1024

FFN_ROWS = 1024
FFN_SUB_ROWS = 512
MIX_ROWS = 256
MIX_SUB_ROWS = 256
BLOCK = 2 * CHUNK
SEG = BLOCK // SUBLANES
BAND = BLOCK + WIN_CHUNKS * CHUNK
HIST = WIN_CHUNKS * CHUNK
LRU_GROUP = MXU_DIM
N_LRU_GROUPS = LRU_WIDTH // LRU_GROUP
BLOCKS_PER_GROUP = LRU_GROUP // LRU_BLOCK
TAPS_BACK = CONV_WIDTH - 1

HEAD_PERM = (0, 4, 1, 5, 2, 6, 3, 7)

BF16 = jnp.bfloat16
F32 = jnp.float32


def _dot(a, b):
    return jnp.dot(a, b, preferred_element_type=F32)


def _dot_nt(a, b):
    return lax.dot_general(a, b, (((1,), (1,)), ((), ())), preferred_element_type=F32)


def _rmsnorm(x, g):
    y = x * lax.rsqrt(jnp.mean(x * x, axis=-1, keepdims=True) + EPS)
    return y * g


def _sigmoid(x):
    return 1.0 / (1.0 + jnp.exp(-x))


def _gelu_tanh(x):
    c = math.sqrt(2.0 / math.pi)
    return 0.5 * x * (1.0 + jnp.tanh(c * (x + 0.044715 * (x * x * x))))


def _resident(shape):
    nd = len(shape)
    return pl.BlockSpec(shape, lambda *_: (0,) * nd, pipeline_mode=pl.Buffered(1))


def _layer_resident(stacked, layer):
    nd = stacked.ndim - 1
    return pl.BlockSpec((None,) + stacked.shape[1:], lambda *_: (layer,) + (0,) * nd,
                        pipeline_mode=pl.Buffered(1))


def _ffn_kernel(x_ref, g_ref, wg_ref, wu_ref, wd_ref, fg_ref, o_ref, *, final_norm):
    for r0 in range(0, FFN_ROWS, FFN_SUB_ROWS):
        rs = slice(r0, r0 + FFN_SUB_ROWS)
        x = x_ref[rs, :]
        h = _rmsnorm(x, g_ref[...]).astype(BF16)
        z = _dot(h, wg_ref[...])
        up = _dot(h, wu_ref[...])
        act = ((z + z * jnp.tanh(z)) * up).astype(BF16)
        y = x + 0.5 * _dot(act, wd_ref[...])
        if final_norm:
            y = _rmsnorm(y, fg_ref[...])
        o_ref[rs, :] = y


def _ffn_call(x2d, layer, g, wg, wu, wd, fg, final_norm):
    t, d = x2d.shape
    rows = pl.BlockSpec((FFN_ROWS, d), lambda i: (i, 0))
    return pl.pallas_call(
        functools.partial(_ffn_kernel, final_norm=final_norm),
        grid=(t // FFN_ROWS,),
        in_specs=[rows, _layer_resident(g, layer), _layer_resident(wg, layer), _layer_resident(wu, layer),
                  _layer_resident(wd, layer), _resident((1, d))],
        out_specs=rows,
        out_shape=jax.ShapeDtypeStruct((t, d), F32),
        compiler_params=pltpu.CompilerParams(
            dimension_semantics=("arbitrary",), vmem_limit_bytes=VMEM_LIMIT_BYTES),
        name="ffn_final" if final_norm else "ffn",
    )(x2d, g, wg, wu, wd, fg)


def _memkv_kernel(mem_ref, g_ref, w_ref, k_ref, v_ref):
    mn = _rmsnorm(mem_ref[0], g_ref[0]).astype(BF16)
    kv = _dot(mn, w_ref[0])
    k_ref[0, 0] = kv[:, :MEM_WIDTH].astype(BF16)
    m = kv.shape[0]
    ones = jnp.ones((m, MEM_HEAD_DIM), BF16)
    for h in range(MEM_HEADS):
        lo = MEM_WIDTH + h * MEM_HEAD_DIM
        v_ref[0, 0, h, :, :MEM_HEAD_DIM] = kv[:, lo:lo + MEM_HEAD_DIM].astype(BF16)
        v_ref[0, 0, h, :, MEM_HEAD_DIM:] = ones


def _memkv_call(mem, mem_norm, w_mem_kv):
    b, m, d = mem.shape
    depth = w_mem_kv.shape[0]
    return pl.pallas_call(
        _memkv_kernel,
        grid=(depth, b),
        in_specs=[pl.BlockSpec((1, m, d), lambda l, i: (i, 0, 0)),
                  pl.BlockSpec((1, 1, d), lambda l, i: (l, 0, 0)),
                  pl.BlockSpec((1, d, 2 * MEM_WIDTH), lambda l, i: (l, 0, 0))],
        out_specs=[pl.BlockSpec((1, 1, m, MEM_WIDTH), lambda l, i: (l, i, 0, 0)),
                   pl.BlockSpec((1, 1, MEM_HEADS, m, 2 * MEM_HEAD_DIM),
                                lambda l, i: (l, i, 0, 0, 0))],
        out_shape=[jax.ShapeDtypeStruct((depth, b, m, MEM_WIDTH), BF16),
                   jax.ShapeDtypeStruct((depth, b, MEM_HEADS, m, 2 * MEM_HEAD_DIM), BF16)],
        compiler_params=pltpu.CompilerParams(
            dimension_semantics=("arbitrary", "arbitrary"), vmem_limit_bytes=VMEM_LIMIT_BYTES),
        name="mem_kv",
    )(mem, mem_norm, w_mem_kv)


def _attn_scores(p, first, j, q, kv_scr, abias_ref):
    lane = lax.broadcasted_iota(jnp.int32, (BLOCK, LANES), 1)
    low = lane < HEAD_DIM
    r0 = p * BLOCK
    parts = []
    for c in range(N_Q_HEADS // 2):
        qc = q[r0:r0 + BLOCK, c * LANES:(c + 1) * LANES]
        parts.append(jnp.where(low, qc, 0.0))
        parts.append(jnp.where(low, 0.0, qc))
    qs = jnp.concatenate(parts, axis=0).astype(BF16)
    kb = kv_scr[r0:r0 + BAND, 0:LANES]
    s = _dot_nt(qs, kb) + abias_ref[...]
    if first:
        col = lax.broadcasted_iota(jnp.int32, (1, BAND), 1)
        neg = jnp.where(j == 0, -jnp.inf, 0.0).astype(F32)
        s = s + jnp.where(col < HIST, neg, 0.0)
    return s


def _attn_probs(s, sink):
    m = jnp.maximum(jnp.max(s, axis=-1, keepdims=True), sink)
    return jnp.exp2(s - m).astype(BF16), jnp.exp2(sink - m)


def _attn_values(p, pr, kv_scr):
    r0 = p * BLOCK
    return _dot(pr, kv_scr[r0:r0 + BAND, LANES:3 * LANES])


def _attn_finish(p, o, esink, att_scr):
    lane = lax.broadcasted_iota(jnp.int32, (BLOCK, LANES), 1)
    low = lane < HEAD_DIM
    r0 = p * BLOCK
    on = o[:, :LANES] / (o[:, LANES:] + esink)
    for c in range(N_Q_HEADS // 2):
        a0 = on[(2 * c) * BLOCK:(2 * c + 1) * BLOCK]
        a1 = on[(2 * c + 1) * BLOCK:(2 * c + 2) * BLOCK]
        att_scr[r0:r0 + BLOCK, c * LANES:(c + 1) * LANES] = jnp.where(low, a0, a1).astype(BF16)


def _mem_scores(h, cq, mk_ref):
    hs = slice(h * MEM_HEAD_DIM, (h + 1) * MEM_HEAD_DIM)
    return _dot_nt(cq[:, hs].astype(BF16), mk_ref[0, 0, :, hs])


def _mem_probs(s):
    return jnp.exp2(s - jnp.max(s, axis=-1, keepdims=True)).astype(BF16)


def _mem_finish(h, o, matt_scr):
    hs = slice(h * MEM_HEAD_DIM, (h + 1) * MEM_HEAD_DIM)
    matt_scr[:, hs] = (o[:, :MEM_HEAD_DIM] / o[:, MEM_HEAD_DIM:]).astype(BF16)


def _lru_conv(xr, ls, tail_scr, cw_ref, cb_ref):
    out = []
    for blk in range(xr.shape[0] // BLOCK):
        x = xr[blk * BLOCK:(blk + 1) * BLOCK]
        tails = []
        for m in range(TAPS_BACK):
            k = SEG - TAPS_BACK + m
            base = 2 * SUBLANES * m
            xk = x[k * SUBLANES:(k + 1) * SUBLANES]
            tail_scr[base + SUBLANES:base + 2 * SUBLANES, ls] = xk
            tails.append(tail_scr[base + SUBLANES - 1:base + 2 * SUBLANES - 1, ls])
            tail_scr[base:base + SUBLANES, ls] = xk
        xe = jnp.concatenate(tails + [x], axis=0)
        off = TAPS_BACK * SUBLANES
        xc = cw_ref[TAPS_BACK:CONV_WIDTH, ls] * x + cb_ref[:, ls]
        for tap in range(TAPS_BACK):
            back = TAPS_BACK - tap
            lo = off - back * SUBLANES
            xc = xc + cw_ref[tap:tap + 1, ls] * xe[lo:lo + BLOCK]
        out.append(xc)
    return jnp.concatenate(out, axis=0)


def _lru_scan(a, b, cin):
    rows = [slice(k * SUBLANES, (k + 1) * SUBLANES) for k in range(SEG)]
    hs = [b[rows[0]]]
    ps = [a[rows[0]]]
    for k in range(1, SEG):
        hs.append(a[rows[k]] * hs[-1] + b[rows[k]])
        ps.append(a[rows[k]] * ps[-1])
    sub = lax.broadcasted_iota(jnp.int32, (SUBLANES, 1), 0)
    ap, hp = ps[-1], hs[-1]
    for d in (1, 2, 4):
        keep = sub >= d
        a_sh = jnp.where(keep, pltpu.roll(ap, d, 0), 1.0)
        h_sh = jnp.where(keep, pltpu.roll(hp, d, 0), 0.0)
        hp = ap * h_sh + hp
        ap = ap * a_sh
    keep = sub >= 1
    ax = jnp.where(keep, pltpu.roll(ap, 1, 0), 1.0)
    hx = jnp.where(keep, pltpu.roll(hp, 1, 0), 0.0)
    c = ax * cin + hx
    h = [hs[k] + ps[k] * c for k in range(SEG)]
    return jnp.concatenate(h, axis=0), h[-1][SUBLANES - 1:SUBLANES]


def _lru_pointwise(gidx, blk, first, j, xc, ga, yr, bah_ref, bxh_ref, lam_ref, hc_scr, lru_scr):
    ls = slice(gidx * LRU_GROUP, (gidx + 1) * LRU_GROUP)
    rs = slice(blk * BLOCK, (blk + 1) * BLOCK)
    t_r = jnp.tanh(ga[rs, :LRU_GROUP] + bah_ref[:, ls])
    t_g = jnp.tanh(ga[rs, LRU_GROUP:] + bxh_ref[:, ls])
    z = -lam_ref[:, ls]
    softplus = jnp.maximum(z, 0.0) + jnp.log1p(jnp.exp(-jnp.abs(z)))
    ch = (-0.5 * LRU_C) * softplus
    a = jnp.exp(ch + ch * t_r)
    mult = jnp.sqrt(jnp.maximum(1.0 - a * a, 0.0))
    if first:
        row = lax.broadcasted_iota(jnp.int32, (SUBLANES, 1), 0)
        first_token = jnp.logical_and(row == 0, j == 0)
        mult = jnp.concatenate([jnp.where(first_token, 1.0, mult[:SUBLANES]), mult[SUBLANES:]], axis=0)
    m1 = mult * (0.5 * xc[rs])
    b = m1 + m1 * t_g
    h, cout = _lru_scan(a, b, hc_scr[:, ls])
    hc_scr[:, ls] = cout
    lru_scr[rs, ls] = (h * _gelu_tanh(yr[rs])).astype(BF16)


def _mix_kernel(x_ref, g_ref, win_ref, gbh_ref, abias_ref, sink_ref, wao_ref,
                cw_ref, cb_ref, wbd_ref, bah_ref, bxh_ref, lam_ref, wlo_ref,
                mk_ref, mv_ref, wmo_ref, wout_ref, o_ref,
                kv_scr, tail_scr, hc_scr, lru_scr, att_scr, matt_scr):
    j = pl.program_id(1)
    ts = MIX_ROWS

    @pl.when(j == 0)
    def _():
        kv_scr[0:HIST, :] = jnp.zeros((HIST, 3 * LANES), BF16)
        kv_scr[:, 2 * LANES:] = jnp.ones((HIST + ts, LANES), BF16)
        tail_scr[...] = jnp.zeros(tail_scr.shape, F32)
        hc_scr[...] = jnp.zeros((1, LRU_WIDTH), F32)

    @pl.when(j > 0)
    def _():
        kv_scr[0:HIST, 0:2 * LANES] = kv_scr[ts:ts + HIST, 0:2 * LANES]

    for u in range(ts // MIX_SUB_ROWS):
        rows = pl.ds(u * MIX_SUB_ROWS, MIX_SUB_ROWS)
        _mix_sub_tile(u == 0, j, x_ref.at[0].at[rows], g_ref, win_ref, gbh_ref, abias_ref, sink_ref, wao_ref,
                      cw_ref, cb_ref, wbd_ref, bah_ref, bxh_ref, lam_ref, wlo_ref,
                      mk_ref, mv_ref, wmo_ref, wout_ref, o_ref.at[0].at[rows],
                      kv_scr.at[pl.ds(u * MIX_SUB_ROWS, HIST + MIX_SUB_ROWS)], tail_scr, hc_scr,
                      lru_scr.at[rows], att_scr.at[rows], matt_scr.at[rows])


def _mix_sub_tile(head, j, x_ref, g_ref, win_ref, gbh_ref, abias_ref, sink_ref, wao_ref,
                  cw_ref, cb_ref, wbd_ref, bah_ref, bxh_ref, lam_ref, wlo_ref,
                  mk_ref, mv_ref, wmo_ref, wout_ref, o_ref,
                  kv_scr, tail_scr, hc_scr, lru_scr, att_scr, matt_scr):
    ts = MIX_SUB_ROWS
    x = x_ref[...]
    hn = _rmsnorm(x, g_ref[...]).astype(BF16)

    sink = sink_ref[...]

    def proj(lo, width):
        return _dot(hn, win_ref[:, lo:lo + width])

    def pxr(g):
        return proj(O_XR + g * LRU_GROUP, LRU_GROUP)

    def pyr(g):
        return proj(O_YR + g * LRU_GROUP, LRU_GROUP)

    def lru_gates(g, xr):
        ls = slice(g * LRU_GROUP, (g + 1) * LRU_GROUP)
        xc = _lru_conv(xr, ls, tail_scr, cw_ref, cb_ref)
        return xc, _dot(xc.astype(BF16), wbd_ref[g])

    def lru_out(g, blk, xc, ga, yr):
        _lru_pointwise(g, blk, head and blk == 0, j, xc, ga, yr, bah_ref, bxh_ref, lam_ref, hc_scr, lru_scr)

    def gate_part(idx, c):
        return proj(O_G + idx * D_MODEL + c * MXU_DIM, MXU_DIM)

    def gated(idx, parts, br):
        t = jnp.tanh(jnp.concatenate(parts, axis=1) + gbh_ref[:, idx * D_MODEL:(idx + 1) * D_MODEL])
        return br + t * br

    att = lambda p: _attn_scores(p, head and p == 0, j, q, kv_scr, abias_ref)

    q = proj(O_Q, ATTN_WIDTH) * (HEAD_DIM ** -0.5 * LOG2E)
    kv = proj(O_KV, 2 * KV_WIDTH)
    kv_scr[HIST:HIST + ts, 0:2 * LANES] = kv.astype(BF16)
    n_parts = D_MODEL // MXU_DIM
    xr0, yr0 = pxr(0), pyr(0)
    s0 = att(0)
    xr1, yr1 = pxr(1), pyr(1)
    s1 = att(1)
    xc0, ga0 = lru_gates(0, xr0)
    p0, es0 = _attn_probs(s0, sink)
    cq = proj(O_CQ, MEM_WIDTH) * (MEM_HEAD_DIM ** -0.5 * LOG2E)
    o0 = _attn_values(0, p0, kv_scr)
    p1, es1 = _attn_probs(s1, sink)
    lru_out(0, 0, xc0, ga0, yr0)
    lru_out(0, 1, xc0, ga0, yr0)
    o1 = _attn_values(1, p1, kv_scr)
    xc1, ga1 = lru_gates(1, xr1)
    ms0 = _mem_scores(0, cq, mk_ref)
    ms1 = _mem_scores(1, cq, mk_ref)
    xr2, yr2 = pxr(2), pyr(2)
    _attn_finish(0, o0, es0, att_scr)
    _attn_finish(1, o1, es1, att_scr)
    lru_out(1, 0, xc1, ga1, yr1)
    lru_out(1, 1, xc1, ga1, yr1)
    mp0 = _mem_probs(ms0)
    mp1 = _mem_probs(ms1)
    br_attn = _dot(att_scr[...], wao_ref[...])
    xc2, ga2 = lru_gates(2, xr2)
    mo0 = _dot(mp0, mv_ref[0, 0, 0])
    mo1 = _dot(mp1, mv_ref[0, 0, 1])
    g0 = [gate_part(0, c) for c in range(n_parts)]
    ms2 = _mem_scores(2, cq, mk_ref)
    ms3 = _mem_scores(3, cq, mk_ref)
    lru_out(2, 0, xc2, ga2, yr2)
    lru_out(2, 1, xc2, ga2, yr2)
    xr3, yr3 = pxr(3), pyr(3)
    merged = gated(0, g0, br_attn)
    _mem_finish(0, mo0, matt_scr)
    _mem_finish(1, mo1, matt_scr)
    mp2 = _mem_probs(ms2)
    mp3 = _mem_probs(ms3)
    xc3, ga3 = lru_gates(3, xr3)
    mo2 = _dot(mp2, mv_ref[0, 0, 2])
    mo3 = _dot(mp3, mv_ref[0, 0, 3])
    g2 = [gate_part(2, c) for c in range(n_parts)]
    lru_out(3, 0, xc3, ga3, yr3)
    lru_out(3, 1, xc3, ga3, yr3)
    _mem_finish(2, mo2, matt_scr)
    _mem_finish(3, mo3, matt_scr)
    g1 = [gate_part(1, c) for c in range(n_parts)]
    br_mem = _dot(matt_scr[...], wmo_ref[...])
    merged = merged + gated(2, g2, br_mem)
    br_lru = _dot(lru_scr[...], wlo_ref[...])
    merged = merged + gated(1, g1, br_lru)

    o_ref[...] = x + _dot(merged.astype(BF16), wout_ref[...])


def _mix_call(x, layer, g, win, gbh, abias, sink, wao, cw, cb, wbd, bah, bxh, lam, wlo,
              memk, memv, wmo, wout):
    b, s, d = x.shape
    ts = MIX_ROWS
    tile = pl.BlockSpec((1, ts, d), lambda i, j: (i, j, 0))
    m = memk.shape[2]
    per_layer = functools.partial(_layer_resident, layer=layer)
    in_specs = [
        tile,
        per_layer(g), per_layer(win), per_layer(gbh), _resident(abias.shape),
        per_layer(sink), per_layer(wao), per_layer(cw), per_layer(cb),
        per_layer(wbd), per_layer(bah), per_layer(bxh), per_layer(lam),
        per_layer(wlo),
        pl.BlockSpec((1, 1, m, MEM_WIDTH), lambda i, j: (layer, i, 0, 0)),
        pl.BlockSpec((1, 1, MEM_HEADS, m, 2 * MEM_HEAD_DIM), lambda i, j: (layer, i, 0, 0, 0)),
        per_layer(wmo), per_layer(wout),
    ]
    scratch = [
        pltpu.VMEM((HIST + ts, 3 * LANES), BF16),
        pltpu.VMEM((2 * SUBLANES * TAPS_BACK, LRU_WIDTH), F32),
        pltpu.VMEM((1, LRU_WIDTH), F32),
        pltpu.VMEM((ts, LRU_WIDTH), BF16),
        pltpu.VMEM((ts, ATTN_WIDTH), BF16),
        pltpu.VMEM((ts, MEM_WIDTH), BF16),
    ]
    return pl.pallas_call(
        _mix_kernel,
        grid=(b, s // ts),
        in_specs=in_specs,
        out_specs=tile,
        out_shape=jax.ShapeDtypeStruct((b, s, d), F32),
        scratch_shapes=scratch,
        compiler_params=pltpu.CompilerParams(
            dimension_semantics=("arbitrary", "arbitrary"), vmem_limit_bytes=VMEM_LIMIT_BYTES),
        name="mixer",
    )(x, g, win, gbh, abias, sink, wao, cw, cb, wbd, bah, bxh, lam, wlo, memk, memv, wmo, wout)


def _row_token():
    r = np.arange(BLOCK)
    return SEG * (r % SUBLANES) + r // SUBLANES


def _alibi_bias():
    tq = _row_token()[:, None]
    c = np.arange(BAND)
    tk = (BLOCK * (c // BLOCK) + _row_token()[c % BLOCK])[None, :]
    dist = np.abs(tq + HIST - tk).astype(np.float32)
    kc = tk // CHUNK
    valid = (kc >= tq // CHUNK) & (kc <= tq // CHUNK + WIN_CHUNKS)
    out = np.zeros((N_Q_HEADS, BLOCK, BAND), np.float32)
    for i, h in enumerate(HEAD_PERM):
        slope = np.float32(2.0 ** (-8.0 * (h + 1) / N_Q_HEADS))
        out[i] = np.where(valid, -(slope * dist) * np.float32(LOG2E), -np.inf)
    return jnp.asarray(out.reshape(N_Q_HEADS * BLOCK, BAND))


def _block_diag_groups(w):
    depth = w.shape[0]
    w = w.reshape(depth, N_LRU_GROUPS, BLOCKS_PER_GROUP, LRU_BLOCK, LRU_BLOCK)
    eye = jnp.eye(BLOCKS_PER_GROUP, dtype=w.dtype)
    bd = jnp.einsum("lgbij,bc->lgbicj", w, eye)
    return bd.reshape(depth, N_LRU_GROUPS, LRU_GROUP, LRU_GROUP)


def _interleave(x, inverse=False):
    b, s, d = x.shape
    shape = (b, s // BLOCK, SEG, SUBLANES, d) if inverse else (b, s // BLOCK, SUBLANES, SEG, d)
    return x.reshape(shape).transpose(0, 1, 3, 2, 4).reshape(b, s, d)


def kernel(x, mem, ffn1_norm, ffn1_w_gate, ffn1_w_up, ffn1_w_down, mix_norm, w_in, gate_bias,
           attn_sinks, w_attn_out, conv_w, conv_b, lru_wa, lru_ba, lru_wx, lru_bx, lru_lambda,
           w_lru_out, mem_norm, w_mem_kv, w_mem_out, w_out, ffn2_norm, ffn2_w_gate, ffn2_w_up,
           ffn2_w_down, final_norm):
    b, s, d = x.shape
    depth = w_in.shape[0]
    assert d == D_MODEL and w_in.shape[2] == IN_WIDTH
    assert s % MIX_ROWS == 0 and (b * s) % FFN_ROWS == 0
    assert MIX_ROWS % MIX_SUB_ROWS == 0 and MIX_SUB_ROWS % BLOCK == 0 and FFN_ROWS % FFN_SUB_ROWS == 0

    def row(v):
        return v.reshape(depth, 1, -1)

    perm = jnp.asarray(HEAD_PERM)
    wq = w_in[:, :, :ATTN_WIDTH].reshape(depth, d, N_Q_HEADS, HEAD_DIM)[:, :, perm].reshape(depth, d, ATTN_WIDTH)
    win = jnp.concatenate([wq, w_in[:, :, ATTN_WIDTH:O_G], 0.5 * w_in[:, :, O_G:]], axis=2).astype(BF16)
    wao = w_attn_out.reshape(depth, N_Q_HEADS, HEAD_DIM, d)[:, perm].reshape(depth, ATTN_WIDTH, d).astype(BF16)
    sink = jnp.repeat(LOG2E * attn_sinks[:, perm], BLOCK, axis=1).reshape(depth, N_Q_HEADS * BLOCK, 1)
    wbd = (0.5 * jnp.concatenate([_block_diag_groups(lru_wa), _block_diag_groups(lru_wx)], axis=3)).astype(BF16)
    mix_params = (row(mix_norm), win, row(0.5 * gate_bias), _alibi_bias(), sink, wao, conv_w, row(conv_b), wbd,
                  row(0.5 * lru_ba), row(0.5 * lru_bx), row(lru_lambda), w_lru_out.astype(BF16))
    wmo, wout = w_mem_out.astype(BF16), (0.5 * w_out).astype(BF16)
    ffn1 = (row(ffn1_norm), (0.5 * ffn1_w_gate).astype(BF16), ffn1_w_up.astype(BF16), ffn1_w_down.astype(BF16))
    ffn2 = (row(ffn2_norm), (0.5 * ffn2_w_gate).astype(BF16), ffn2_w_up.astype(BF16), ffn2_w_down.astype(BF16))
    fg = final_norm.reshape(1, d)

    memk, memv = _memkv_call(mem, row(mem_norm), w_mem_kv.astype(BF16))
    x = _interleave(x)
    for l in range(depth):
        x = _ffn_call(x.reshape(b * s, d), l, *ffn1, fg, False).reshape(b, s, d)
        x = _mix_call(x, l, *mix_params, memk, memv, wmo, wout)
        x = _ffn_call(x.reshape(b * s, d), l, *ffn2, fg, l == depth - 1).reshape(b, s, d)
    return _interleave(x, inverse=True)
```

```python
import functools
import math

import numpy as np
import jax
import jax.numpy as jnp
from jax import lax
from jax.experimental import pallas as pl
from jax.experimental.pallas import tpu as pltpu

D_MODEL = 1024
CHUNK = 64
WIN_CHUNKS = 2
HEAD_DIM = 64
N_Q_HEADS = 8
N_KV_HEADS = 2
ATTN_WIDTH = N_Q_HEADS * HEAD_DIM
KV_WIDTH = N_KV_HEADS * HEAD_DIM
LRU_WIDTH = D_MODEL
LRU_BLOCKS = 16
LRU_BLOCK = LRU_WIDTH // LRU_BLOCKS
CONV_WIDTH = 4
LRU_C = 8.0
MEM_HEADS = 4
MEM_WIDTH = D_MODEL // 2
MEM_HEAD_DIM = MEM_WIDTH // MEM_HEADS
N_BRANCH = 3
EPS = 1e-6
LOG2E = math.log2(math.e)

O_Q = 0
O_KV = O_Q + ATTN_WIDTH
O_XR = O_KV + 2 * KV_WIDTH
O_YR = O_XR + LRU_WIDTH
O_CQ = O_YR + LRU_WIDTH
O_G = O_CQ + MEM_WIDTH
IN_WIDTH = O_G + N_BRANCH * D_MODEL

LANES = 128
SUBLANES = 8
MXU_DIM = 256
VMEM_LIMIT_BYTES = 56 * 1024 * 1024

FFN_ROWS = 1024
FFN_SUB_ROWS = 512
MIX_ROWS = 256
MIX_SUB_ROWS = 256
BLOCK = 2 * CHUNK
SEG = BLOCK // SUBLANES
BAND = BLOCK + WIN_CHUNKS * CHUNK
HIST = WIN_CHUNKS * CHUNK
LRU_GROUP = MXU_DIM
N_LRU_GROUPS = LRU_WIDTH // LRU_GROUP
BLOCKS_PER_GROUP = LRU_GROUP // LRU_BLOCK
TAPS_BACK = CONV_WIDTH - 1

HEAD_PERM = (0, 4, 1, 5, 2, 6, 3, 7)

BF16 = jnp.bfloat16
F32 = jnp.float32


def _dot(a, b):
    return jnp.dot(a, b, preferred_element_type=F32)


def _dot_nt(a, b):
    return lax.dot_general(a, b, (((1,), (1,)), ((), ())), preferred_element_type=F32)


def _rmsnorm(x, g):
    y = x * lax.rsqrt(jnp.mean(x * x, axis=-1, keepdims=True) + EPS)
    return y * g


def _gelu_tanh(x):
    c = math.sqrt(2.0 / math.pi)
    return 0.5 * x * (1.0 + jnp.tanh(c * (x + 0.044715 * (x * x * x))))


def _resident(shape):
    nd = len(shape)
    return pl.BlockSpec(shape, lambda *_: (0,) * nd, pipeline_mode=pl.Buffered(1))


def _layer_resident(stacked, layer):
    nd = stacked.ndim - 1
    return pl.BlockSpec((None,) + stacked.shape[1:], lambda *_: (layer,) + (0,) * nd,
                        pipeline_mode=pl.Buffered(1))


def _ffn_kernel(x_ref, g_ref, wg_ref, wu_ref, wd_ref, fg_ref, o_ref, *, final_norm):
    for r0 in range(0, FFN_ROWS, FFN_SUB_ROWS):
        rs = slice(r0, r0 + FFN_SUB_ROWS)
        x = x_ref[rs, :]
        h = _rmsnorm(x, g_ref[...]).astype(BF16)
        z = _dot(h, wg_ref[...])
        up = _dot(h, wu_ref[...])
        act = ((z + z * jnp.tanh(z)) * up).astype(BF16)
        y = x + _dot(act, wd_ref[...])
        if final_norm:
            y = _rmsnorm(y, fg_ref[...])
        o_ref[rs, :] = y


def _ffn_call(x2d, layer, g, wg, wu, wd, fg, final_norm):
    t, d = x2d.shape
    rows = pl.BlockSpec((FFN_ROWS, d), lambda i: (i, 0))
    return pl.pallas_call(
        functools.partial(_ffn_kernel, final_norm=final_norm),
        grid=(t // FFN_ROWS,),
        in_specs=[rows, _layer_resident(g, layer), _layer_resident(wg, layer), _layer_resident(wu, layer),
                  _layer_resident(wd, layer), _resident((1, d))],
        out_specs=rows,
        out_shape=jax.ShapeDtypeStruct((t, d), F32),
        compiler_params=pltpu.CompilerParams(
            dimension_semantics=("arbitrary",), vmem_limit_bytes=VMEM_LIMIT_BYTES),
        name="ffn_final" if final_norm else "ffn",
    )(x2d, g, wg, wu, wd, fg)


def _memkv_kernel(mem_ref, g_ref, w_ref, k_ref, v_ref):
    mn = _rmsnorm(mem_ref[0], g_ref[0]).astype(BF16)
    kv = _dot(mn, w_ref[0])
    k_ref[0, 0] = kv[:, :MEM_WIDTH].astype(BF16)
    m = kv.shape[0]
    ones = jnp.ones((m, MEM_HEAD_DIM), BF16)
    for h in range(MEM_HEADS):
        lo = MEM_WIDTH + h * MEM_HEAD_DIM
        v_ref[0, 0, h, :, :MEM_HEAD_DIM] = kv[:, lo:lo + MEM_HEAD_DIM].astype(BF16)
        v_ref[0, 0, h, :, MEM_HEAD_DIM:] = ones


def _memkv_call(mem, mem_norm, w_mem_kv):
    b, m, d = mem.shape
    depth = w_mem_kv.shape[0]
    return pl.pallas_call(
        _memkv_kernel,
        grid=(depth, b),
        in_specs=[pl.BlockSpec((1, m, d), lambda l, i: (i, 0, 0)),
                  pl.BlockSpec((1, 1, d), lambda l, i: (l, 0, 0)),
                  pl.BlockSpec((1, d, 2 * MEM_WIDTH), lambda l, i: (l, 0, 0))],
        out_specs=[pl.BlockSpec((1, 1, m, MEM_WIDTH), lambda l, i: (l, i, 0, 0)),
                   pl.BlockSpec((1, 1, MEM_HEADS, m, 2 * MEM_HEAD_DIM),
                                lambda l, i: (l, i, 0, 0, 0))],
        out_shape=[jax.ShapeDtypeStruct((depth, b, m, MEM_WIDTH), BF16),
                   jax.ShapeDtypeStruct((depth, b, MEM_HEADS, m, 2 * MEM_HEAD_DIM), BF16)],
        compiler_params=pltpu.CompilerParams(
            dimension_semantics=("arbitrary", "arbitrary"), vmem_limit_bytes=VMEM_LIMIT_BYTES),
        name="mem_kv",
    )(mem, mem_norm, w_mem_kv)


def _attn_scores(p, first, j, q, kv_scr, abias_ref):
    lane = lax.broadcasted_iota(jnp.int32, (BLOCK, LANES), 1)
    low = lane < HEAD_DIM
    r0 = p * BLOCK
    parts = []
    for c in range(N_Q_HEADS // 2):
        qc = q[r0:r0 + BLOCK, c * LANES:(c + 1) * LANES]
        parts.append(jnp.where(low, qc, 0.0))
        parts.append(jnp.where(low, 0.0, qc))
    qs = jnp.concatenate(parts, axis=0).astype(BF16)
    kb = kv_scr[r0:r0 + BAND, 0:LANES]
    s = _dot_nt(qs, kb) + abias_ref[...]
    if first:
        col = lax.broadcasted_iota(jnp.int32, (1, BAND), 1)
        neg = jnp.where(j == 0, -jnp.inf, 0.0).astype(F32)
        s = s + jnp.where(col < HIST, neg, 0.0)
    return s


def _attn_probs(s, sink):
    m = jnp.maximum(jnp.max(s, axis=-1, keepdims=True), sink)
    return jnp.exp2(s - m).astype(BF16), jnp.exp2(sink - m)


def _attn_values(p, pr, kv_scr):
    r0 = p * BLOCK
    return _dot(pr, kv_scr[r0:r0 + BAND, LANES:3 * LANES])


def _attn_finish(p, o, esink, att_scr):
    lane = lax.broadcasted_iota(jnp.int32, (BLOCK, LANES), 1)
    low = lane < HEAD_DIM
    r0 = p * BLOCK
    on = o[:, :LANES] / (o[:, LANES:] + esink)
    for c in range(N_Q_HEADS // 2):
        a0 = on[(2 * c) * BLOCK:(2 * c + 1) * BLOCK]
        a1 = on[(2 * c + 1) * BLOCK:(2 * c + 2) * BLOCK]
        att_scr[r0:r0 + BLOCK, c * LANES:(c + 1) * LANES] = jnp.where(low, a0, a1).astype(BF16)


def _mem_scores(h, cq, mk_ref):
    hs = slice(h * MEM_HEAD_DIM, (h + 1) * MEM_HEAD_DIM)
    return _dot_nt(cq[:, hs].astype(BF16), mk_ref[0, 0, :, hs])


def _mem_probs(s):
    return jnp.exp2(s - jnp.max(s, axis=-1, keepdims=True)).astype(BF16)


def _mem_finish(h, o, matt_scr):
    hs = slice(h * MEM_HEAD_DIM, (h + 1) * MEM_HEAD_DIM)
    matt_scr[:, hs] = (o[:, :MEM_HEAD_DIM] / o[:, MEM_HEAD_DIM:]).astype(BF16)


def _lru_conv(xr, ls, tail_scr, cw_ref, cb_ref):
    out = []
    for blk in range(xr.shape[0] // BLOCK):
        x = xr[blk * BLOCK:(blk + 1) * BLOCK]
        tails = []
        for m in range(TAPS_BACK):
            k = SEG - TAPS_BACK + m
            base = 2 * SUBLANES * m
            xk = x[k * SUBLANES:(k + 1) * SUBLANES]
            tail_scr[base + SUBLANES:base + 2 * SUBLANES, ls] = xk
            tails.append(tail_scr[base + SUBLANES - 1:base + 2 * SUBLANES - 1, ls])
            tail_scr[base:base + SUBLANES, ls] = xk
        xe = jnp.concatenate(tails + [x], axis=0)
        off = TAPS_BACK * SUBLANES
        xc = cw_ref[TAPS_BACK:CONV_WIDTH, ls] * x + cb_ref[:, ls]
        for tap in range(TAPS_BACK):
            back = TAPS_BACK - tap
            lo = off - back * SUBLANES
            xc = xc + cw_ref[tap:tap + 1, ls] * xe[lo:lo + BLOCK]
        out.append(xc)
    return jnp.concatenate(out, axis=0)


def _lru_scan(a, b, cin):
    rows = [slice(k * SUBLANES, (k + 1) * SUBLANES) for k in range(SEG)]
    hs = [b[rows[0]]]
    ps = [a[rows[0]]]
    for k in range(1, SEG):
        hs.append(a[rows[k]] * hs[-1] + b[rows[k]])
        ps.append(a[rows[k]] * ps[-1])
    sub = lax.broadcasted_iota(jnp.int32, (SUBLANES, 1), 0)
    ap, hp = ps[-1], hs[-1]
    for d in (1, 2, 4):
        keep = sub >= d
        a_sh = jnp.where(keep, pltpu.roll(ap, d, 0), 1.0)
        h_sh = jnp.where(keep, pltpu.roll(hp, d, 0), 0.0)
        hp = ap * h_sh + hp
        ap = ap * a_sh
    keep = sub >= 1
    ax = jnp.where(keep, pltpu.roll(ap, 1, 0), 1.0)
    hx = jnp.where(keep, pltpu.roll(hp, 1, 0), 0.0)
    c = ax * cin + hx
    h = [hs[k] + ps[k] * c for k in range(SEG)]
    return jnp.concatenate(h, axis=0), h[-1][SUBLANES - 1:SUBLANES]


def _lru_pointwise(gidx, blk, first, j, xc, ga, yr, bah_ref, bxh_ref, lam_ref, hc_scr, lru_scr):
    ls = slice(gidx * LRU_GROUP, (gidx + 1) * LRU_GROUP)
    rs = slice(blk * BLOCK, (blk + 1) * BLOCK)
    t_r = jnp.tanh(ga[rs, :LRU_GROUP] + bah_ref[:, ls])
    t_g = jnp.tanh(ga[rs, LRU_GROUP:] + bxh_ref[:, ls])
    z = -lam_ref[:, ls]
    softplus = jnp.maximum(z, 0.0) + jnp.log1p(jnp.exp(-jnp.abs(z)))
    ch = (-0.5 * LRU_C) * softplus
    a = jnp.exp(ch + ch * t_r)
    mult = jnp.sqrt(jnp.maximum(1.0 - a * a, 0.0))
    if first:
        row = lax.broadcasted_iota(jnp.int32, (SUBLANES, 1), 0)
        first_token = jnp.logical_and(row == 0, j == 0)
        mult = jnp.concatenate([jnp.where(first_token, 1.0, mult[:SUBLANES]), mult[SUBLANES:]], axis=0)
    m1 = mult * (0.5 * xc[rs])
    b = m1 + m1 * t_g
    h, cout = _lru_scan(a, b, hc_scr[:, ls])
    hc_scr[:, ls] = cout
    lru_scr[rs, ls] = (h * _gelu_tanh(yr[rs])).astype(BF16)


def _mix_kernel(x_ref, g_ref, win_ref, gbh_ref, abias_ref, sink_ref, wao_ref,
                cw_ref, cb_ref, wbd_ref, bah_ref, bxh_ref, lam_ref, wlo_ref,
                mk_ref, mv_ref, wmo_ref, wout_ref, o_ref,
                kv_scr, tail_scr, hc_scr, lru_scr, att_scr, matt_scr):
    j = pl.program_id(1)
    ts = MIX_ROWS

    @pl.when(j == 0)
    def _():
        kv_scr[0:HIST, :] = jnp.zeros((HIST, 3 * LANES), BF16)
        kv_scr[:, 2 * LANES:] = jnp.ones((HIST + ts, LANES), BF16)
        tail_scr[...] = jnp.zeros(tail_scr.shape, F32)
        hc_scr[...] = jnp.zeros((1, LRU_WIDTH), F32)

    @pl.when(j > 0)
    def _():
        kv_scr[0:HIST, 0:2 * LANES] = kv_scr[ts:ts + HIST, 0:2 * LANES]

    for u in range(ts // MIX_SUB_ROWS):
        rows = pl.ds(u * MIX_SUB_ROWS, MIX_SUB_ROWS)
        _mix_sub_tile(u == 0, j, x_ref.at[0].at[rows], g_ref, win_ref, gbh_ref, abias_ref, sink_ref, wao_ref,
                      cw_ref, cb_ref, wbd_ref, bah_ref, bxh_ref, lam_ref, wlo_ref,
                      mk_ref, mv_ref, wmo_ref, wout_ref, o_ref.at[0].at[rows],
                      kv_scr.at[pl.ds(u * MIX_SUB_ROWS, HIST + MIX_SUB_ROWS)], tail_scr, hc_scr,
                      lru_scr.at[rows], att_scr.at[rows], matt_scr.at[rows])


def _mix_sub_tile(head, j, x_ref, g_ref, win_ref, gbh_ref, abias_ref, sink_ref, wao_ref,
                  cw_ref, cb_ref, wbd_ref, bah_ref, bxh_ref, lam_ref, wlo_ref,
                  mk_ref, mv_ref, wmo_ref, wout_ref, o_ref,
                  kv_scr, tail_scr, hc_scr, lru_scr, att_scr, matt_scr):
    ts = MIX_SUB_ROWS
    x = x_ref[...]
    hn = _rmsnorm(x, g_ref[...]).astype(BF16)

    sink = sink_ref[...]

    def proj(lo, width):
        return _dot(hn, win_ref[:, lo:lo + width])

    def pxr(g):
        return proj(O_XR + g * LRU_GROUP, LRU_GROUP)

    def pyr(g):
        return proj(O_YR + g * LRU_GROUP, LRU_GROUP)

    def lru_gates(g, xr):
        ls = slice(g * LRU_GROUP, (g + 1) * LRU_GROUP)
        xc = _lru_conv(xr, ls, tail_scr, cw_ref, cb_ref)
        return xc, _dot(xc.astype(BF16), wbd_ref[g])

    def lru_out(g, blk, xc, ga, yr):
        _lru_pointwise(g, blk, head and blk == 0, j, xc, ga, yr, bah_ref, bxh_ref, lam_ref, hc_scr, lru_scr)

    def gate_part(idx, c):
        return proj(O_G + idx * D_MODEL + c * MXU_DIM, MXU_DIM)

    def gated(idx, parts, br):
        t = jnp.tanh(jnp.concatenate(parts, axis=1) + gbh_ref[:, idx * D_MODEL:(idx + 1) * D_MODEL])
        return br + t * br

    att = lambda p: _attn_scores(p, head and p == 0, j, q, kv_scr, abias_ref)

    q = proj(O_Q, ATTN_WIDTH) * (HEAD_DIM ** -0.5 * LOG2E)
    kv = proj(O_KV, 2 * KV_WIDTH)
    kv_scr[HIST:HIST + ts, 0:2 * LANES] = kv.astype(BF16)
    n_parts = D_MODEL // MXU_DIM
    xr0, yr0 = pxr(0), pyr(0)
    s0 = att(0)
    xr1, yr1 = pxr(1), pyr(1)
    s1 = att(1)
    xc0, ga0 = lru_gates(0, xr0)
    p0, es0 = _attn_probs(s0, sink)
    cq = proj(O_CQ, MEM_WIDTH) * (MEM_HEAD_DIM ** -0.5 * LOG2E)
    o0 = _attn_values(0, p0, kv_scr)
    p1, es1 = _attn_probs(s1, sink)
    lru_out(0, 0, xc0, ga0, yr0)
    lru_out(0, 1, xc0, ga0, yr0)
    o1 = _attn_values(1, p1, kv_scr)
    xc1, ga1 = lru_gates(1, xr1)
    ms0 = _mem_scores(0, cq, mk_ref)
    ms1 = _mem_scores(1, cq, mk_ref)
    xr2, yr2 = pxr(2), pyr(2)
    _attn_finish(0, o0, es0, att_scr)
    _attn_finish(1, o1, es1, att_scr)
    lru_out(1, 0, xc1, ga1, yr1)
    lru_out(1, 1, xc1, ga1, yr1)
    mp0 = _mem_probs(ms0)
    mp1 = _mem_probs(ms1)
    br_attn = _dot(att_scr[...], wao_ref[...])
    xc2, ga2 = lru_gates(2, xr2)
    mo0 = _dot(mp0, mv_ref[0, 0, 0])
    mo1 = _dot(mp1, mv_ref[0, 0, 1])
    g0 = [gate_part(0, c) for c in range(n_parts)]
    ms2 = _mem_scores(2, cq, mk_ref)
    ms3 = _mem_scores(3, cq, mk_ref)
    lru_out(2, 0, xc2, ga2, yr2)
    lru_out(2, 1, xc2, ga2, yr2)
    xr3, yr3 = pxr(3), pyr(3)
    merged = gated(0, g0, br_attn)
    _mem_finish(0, mo0, matt_scr)
    _mem_finish(1, mo1, matt_scr)
    mp2 = _mem_probs(ms2)
    mp3 = _mem_probs(ms3)
    xc3, ga3 = lru_gates(3, xr3)
    mo2 = _dot(mp2, mv_ref[0, 0, 2])
    mo3 = _dot(mp3, mv_ref[0, 0, 3])
    g2 = [gate_part(2, c) for c in range(n_parts)]
    lru_out(3, 0, xc3, ga3, yr3)
    lru_out(3, 1, xc3, ga3, yr3)
    _mem_finish(2, mo2, matt_scr)
    _mem_finish(3, mo3, matt_scr)
    g1 = [gate_part(1, c) for c in range(n_parts)]
    br_mem = _dot(matt_scr[...], wmo_ref[...])
    merged = merged + gated(2, g2, br_mem)
    br_lru = _dot(lru_scr[...], wlo_ref[...])
    merged = merged + gated(1, g1, br_lru)

    o_ref[...] = x + _dot(merged.astype(BF16), wout_ref[...])


def _mix_call(x, layer, g, win, gbh, abias, sink, wao, cw, cb, wbd, bah, bxh, lam, wlo,
              memk, memv, wmo, wout):
    b, s, d = x.shape
    ts = MIX_ROWS
    tile = pl.BlockSpec((1, ts, d), lambda i, j: (i, j, 0))
    m = memk.shape[2]
    per_layer = functools.partial(_layer_resident, layer=layer)
    in_specs = [
        tile,
        per_layer(g), per_layer(win), per_layer(gbh), _resident(abias.shape),
        per_layer(sink), per_layer(wao), per_layer(cw), per_layer(cb),
        per_layer(wbd), per_layer(bah), per_layer(bxh), per_layer(lam),
        per_layer(wlo),
        pl.BlockSpec((1, 1, m, MEM_WIDTH), lambda i, j: (layer, i, 0, 0)),
        pl.BlockSpec((1, 1, MEM_HEADS, m, 2 * MEM_HEAD_DIM), lambda i, j: (layer, i, 0, 0, 0)),
        per_layer(wmo), per_layer(wout),
    ]
    scratch = [
        pltpu.VMEM((HIST + ts, 3 * LANES), BF16),
        pltpu.VMEM((2 * SUBLANES * TAPS_BACK, LRU_WIDTH), F32),
        pltpu.VMEM((1, LRU_WIDTH), F32),
        pltpu.VMEM((ts, LRU_WIDTH), BF16),
        pltpu.VMEM((ts, ATTN_WIDTH), BF16),
        pltpu.VMEM((ts, MEM_WIDTH), BF16),
    ]
    return pl.pallas_call(
        _mix_kernel,
        grid=(b, s // ts),
        in_specs=in_specs,
        out_specs=tile,
        out_shape=jax.ShapeDtypeStruct((b, s, d), F32),
        scratch_shapes=scratch,
        compiler_params=pltpu.CompilerParams(
            dimension_semantics=("arbitrary", "arbitrary"), vmem_limit_bytes=VMEM_LIMIT_BYTES),
        name="mixer",
    )(x, g, win, gbh, abias, sink, wao, cw, cb, wbd, bah, bxh, lam, wlo, memk, memv, wmo, wout)


def _row_token():
    r = np.arange(BLOCK)
    return SEG * (r % SUBLANES) + r // SUBLANES


def _alibi_bias():
    tq = _row_token()[:, None]
    c = np.arange(BAND)
    tk = (BLOCK * (c // BLOCK) + _row_token()[c % BLOCK])[None, :]
    dist = np.abs(tq + HIST - tk).astype(np.float32)
    kc = tk // CHUNK
    valid = (kc >= tq // CHUNK) & (kc <= tq // CHUNK + WIN_CHUNKS)
    out = np.zeros((N_Q_HEADS, BLOCK, BAND), np.float32)
    for i, h in enumerate(HEAD_PERM):
        slope = np.float32(2.0 ** (-8.0 * (h + 1) / N_Q_HEADS))
        out[i] = np.where(valid, -(slope * dist) * np.float32(LOG2E), -np.inf)
    return jnp.asarray(out.reshape(N_Q_HEADS * BLOCK, BAND))


def _block_diag_groups(w):
    depth = w.shape[0]
    w = w.reshape(depth, N_LRU_GROUPS, BLOCKS_PER_GROUP, LRU_BLOCK, LRU_BLOCK)
    eye = jnp.eye(BLOCKS_PER_GROUP, dtype=w.dtype)
    bd = jnp.einsum("lgbij,bc->lgbicj", w, eye)
    return bd.reshape(depth, N_LRU_GROUPS, LRU_GROUP, LRU_GROUP)


def _interleave(x, inverse=False):
    b, s, d = x.shape
    shape = (b, s // BLOCK, SEG, SUBLANES, d) if inverse else (b, s // BLOCK, SUBLANES, SEG, d)
    return x.reshape(shape).transpose(0, 1, 3, 2, 4).reshape(b, s, d)


def kernel(x, mem, ffn1_norm, ffn1_w_gate, ffn1_w_up, ffn1_w_down, mix_norm, w_in, gate_bias,
           attn_sinks, w_attn_out, conv_w, conv_b, lru_wa, lru_ba, lru_wx, lru_bx, lru_lambda,
           w_lru_out, mem_norm, w_mem_kv, w_mem_out, w_out, ffn2_norm, ffn2_w_gate, ffn2_w_up,
           ffn2_w_down, final_norm):
    b, s, d = x.shape
    depth = w_in.shape[0]
    assert d == D_MODEL and w_in.shape[2] == IN_WIDTH
    assert s % MIX_ROWS == 0 and (b * s) % FFN_ROWS == 0
    assert MIX_ROWS % MIX_SUB_ROWS == 0 and MIX_SUB_ROWS % BLOCK == 0 and FFN_ROWS % FFN_SUB_ROWS == 0

    def row(v):
        return v.reshape(depth, 1, -1)

    perm = jnp.asarray(HEAD_PERM)
    wq = w_in[:, :, :ATTN_WIDTH].reshape(depth, d, N_Q_HEADS, HEAD_DIM)[:, :, perm].reshape(depth, d, ATTN_WIDTH)
    win = jnp.concatenate([wq, w_in[:, :, ATTN_WIDTH:O_G], 0.5 * w_in[:, :, O_G:]], axis=2).astype(BF16)
    wao = w_attn_out.reshape(depth, N_Q_HEADS, HEAD_DIM, d)[:, perm].reshape(depth, ATTN_WIDTH, d).astype(BF16)
    sink = jnp.repeat(LOG2E * attn_sinks[:, perm], BLOCK, axis=1).reshape(depth, N_Q_HEADS * BLOCK, 1)
    wbd = (0.5 * jnp.concatenate([_block_diag_groups(lru_wa), _block_diag_groups(lru_wx)], axis=3)).astype(BF16)
    mix_params = (row(mix_norm), win, row(0.5 * gate_bias), _alibi_bias(), sink, wao, conv_w, row(conv_b), wbd,
                  row(0.5 * lru_ba), row(0.5 * lru_bx), row(lru_lambda), w_lru_out.astype(BF16))
    wmo, wout = w_mem_out.astype(BF16), (0.5 * w_out).astype(BF16)
    half = lambda w: (0.5 * w).astype(BF16)
    ffn1 = (row(ffn1_norm), half(ffn1_w_gate), ffn1_w_up.astype(BF16), half(ffn1_w_down))
    ffn2 = (row(ffn2_norm), half(ffn2_w_gate), ffn2_w_up.astype(BF16), half(ffn2_w_down))
    fg = final_norm.reshape(1, d)

    memk, memv = _memkv_call(mem, row(mem_norm), w_mem_kv.astype(BF16))
    x = _interleave(x)
    for l in range(depth):
        x = _ffn_call(x.reshape(b * s, d), l, *ffn1, fg, False).reshape(b, s, d)
        x = _mix_call(x, l, *mix_params, memk, memv, wmo, wout)
        x = _ffn_call(x.reshape(b * s, d), l, *ffn2, fg, l == depth - 1).reshape(b, s, d)
    return _interleave(x, inverse=True)
```

```python
import functools
import math

import numpy as np
import jax
import jax.numpy as jnp
from jax import lax
from jax.experimental import pallas as pl
from jax.experimental.pallas import tpu as pltpu

D_MODEL = 1024
CHUNK = 64
WIN_CHUNKS = 2
HEAD_DIM = 64
N_Q_HEADS = 8
N_KV_HEADS = 2
ATTN_WIDTH = N_Q_HEADS * HEAD_DIM
KV_WIDTH = N_KV_HEADS * HEAD_DIM
LRU_WIDTH = D_MODEL
LRU_BLOCKS = 16
LRU_BLOCK = LRU_WIDTH // LRU_BLOCKS
CONV_WIDTH = 4
LRU_C = 8.0
MEM_HEADS = 4
MEM_WIDTH = D_MODEL // 2
MEM_HEAD_DIM = MEM_WIDTH // MEM_HEADS
N_BRANCH = 3
EPS = 1e-6
LOG2E = math.log2(math.e)

O_Q = 0
O_KV = O_Q + ATTN_WIDTH
O_XR = O_KV + 2 * KV_WIDTH
O_YR = O_XR + LRU_WIDTH
O_CQ = O_YR + LRU_WIDTH
O_G = O_CQ + MEM_WIDTH
IN_WIDTH = O_G + N_BRANCH * D_MODEL

LANES = 128
SUBLANES = 8
MXU_DIM = 256
VMEM_LIMIT_BYTES = 56 * 1024 * 1024

FFN_ROWS = 1024
FFN_SUB_ROWS = 256
MIX_ROWS = 256
MIX_SUB_ROWS = 256
BLOCK = 2 * CHUNK
SEG = BLOCK // SUBLANES
BAND = BLOCK + WIN_CHUNKS * CHUNK
HIST = WIN_CHUNKS * CHUNK
LRU_GROUP = MXU_DIM
N_LRU_GROUPS = LRU_WIDTH // LRU_GROUP
BLOCKS_PER_GROUP = LRU_GROUP // LRU_BLOCK
TAPS_BACK = CONV_WIDTH - 1

HEAD_PERM = (0, 4, 1, 5, 2, 6, 3, 7)

BF16 = jnp.bfloat16
F32 = jnp.float32


def _dot(a, b):
    return jnp.dot(a, b, preferred_element_type=F32)


def _dot_nt(a, b):
    return lax.dot_general(a, b, (((1,), (1,)), ((), ())), preferred_element_type=F32)


def _rmsnorm(x, g):
    y = x * lax.rsqrt(jnp.mean(x * x, axis=-1, keepdims=True) + EPS)
    return y * g


def _gelu_tanh(x):
    c = math.sqrt(2.0 / math.pi)
    return 0.5 * x * (1.0 + jnp.tanh(c * (x + 0.044715 * (x * x * x))))


def _resident(shape):
    nd = len(shape)
    return pl.BlockSpec(shape, lambda *_: (0,) * nd, pipeline_mode=pl.Buffered(1))


def _layer_resident(stacked, layer):
    nd = stacked.ndim - 1
    return pl.BlockSpec((None,) + stacked.shape[1:], lambda *_: (layer,) + (0,) * nd,
                        pipeline_mode=pl.Buffered(1))


def _ffn_kernel(x_ref, g_ref, wg_ref, wu_ref, wd_ref, fg_ref, o_ref, *, final_norm):
    for r0 in range(0, FFN_ROWS, FFN_SUB_ROWS):
        rs = slice(r0, r0 + FFN_SUB_ROWS)
        x = x_ref[rs, :]
        h = _rmsnorm(x, g_ref[...]).astype(BF16)
        z = _dot(h, wg_ref[...])
        up = _dot(h, wu_ref[...])
        act = ((z + z * jnp.tanh(z)) * up).astype(BF16)
        y = x + _dot(act, wd_ref[...])
        if final_norm:
            y = _rmsnorm(y, fg_ref[...])
        o_ref[rs, :] = y


def _ffn_call(x2d, layer, g, wg, wu, wd, fg, final_norm):
    t, d = x2d.shape
    rows = pl.BlockSpec((FFN_ROWS, d), lambda i: (i, 0))
    return pl.pallas_call(
        functools.partial(_ffn_kernel, final_norm=final_norm),
        grid=(t // FFN_ROWS,),
        in_specs=[rows, _layer_resident(g, layer), _layer_resident(wg, layer), _layer_resident(wu, layer),
                  _layer_resident(wd, layer), _resident((1, d))],
        out_specs=rows,
        out_shape=jax.ShapeDtypeStruct((t, d), F32),
        compiler_params=pltpu.CompilerParams(
            dimension_semantics=("arbitrary",), vmem_limit_bytes=VMEM_LIMIT_BYTES),
        name="ffn_final" if final_norm else "ffn",
    )(x2d, g, wg, wu, wd, fg)


def _memkv_kernel(mem_ref, g_ref, w_ref, k_ref, v_ref):
    mn = _rmsnorm(mem_ref[0], g_ref[0]).astype(BF16)
    kv = _dot(mn, w_ref[0])
    k_ref[0, 0] = kv[:, :MEM_WIDTH].astype(BF16)
    m = kv.shape[0]
    ones = jnp.ones((m, MEM_HEAD_DIM), BF16)
    for h in range(MEM_HEADS):
        lo = MEM_WIDTH + h * MEM_HEAD_DIM
        v_ref[0, 0, h, :, :MEM_HEAD_DIM] = kv[:, lo:lo + MEM_HEAD_DIM].astype(BF16)
        v_ref[0, 0, h, :, MEM_HEAD_DIM:] = ones


def _memkv_call(mem, mem_norm, w_mem_kv):
    b, m, d = mem.shape
    depth = w_mem_kv.shape[0]
    return pl.pallas_call(
        _memkv_kernel,
        grid=(depth, b),
        in_specs=[pl.BlockSpec((1, m, d), lambda l, i: (i, 0, 0)),
                  pl.BlockSpec((1, 1, d), lambda l, i: (l, 0, 0)),
                  pl.BlockSpec((1, d, 2 * MEM_WIDTH), lambda l, i: (l, 0, 0))],
        out_specs=[pl.BlockSpec((1, 1, m, MEM_WIDTH), lambda l, i: (l, i, 0, 0)),
                   pl.BlockSpec((1, 1, MEM_HEADS, m, 2 * MEM_HEAD_DIM),
                                lambda l, i: (l, i, 0, 0, 0))],
        out_shape=[jax.ShapeDtypeStruct((depth, b, m, MEM_WIDTH), BF16),
                   jax.ShapeDtypeStruct((depth, b, MEM_HEADS, m, 2 * MEM_HEAD_DIM), BF16)],
        compiler_params=pltpu.CompilerParams(
            dimension_semantics=("arbitrary", "arbitrary"), vmem_limit_bytes=VMEM_LIMIT_BYTES),
        name="mem_kv",
    )(mem, mem_norm, w_mem_kv)


def _attn_scores(p, first, j, q, kv_scr, abias_ref):
    lane = lax.broadcasted_iota(jnp.int32, (BLOCK, LANES), 1)
    low = lane < HEAD_DIM
    r0 = p * BLOCK
    parts = []
    for c in range(N_Q_HEADS // 2):
        qc = q[r0:r0 + BLOCK, c * LANES:(c + 1) * LANES]
        parts.append(jnp.where(low, qc, 0.0))
        parts.append(jnp.where(low, 0.0, qc))
    qs = jnp.concatenate(parts, axis=0).astype(BF16)
    kb = kv_scr[r0:r0 + BAND, 0:LANES]
    s = _dot_nt(qs, kb) + abias_ref[...]
    if first:
        col = lax.broadcasted_iota(jnp.int32, (1, BAND), 1)
        neg = jnp.where(j == 0, -jnp.inf, 0.0).astype(F32)
        s = s + jnp.where(col < HIST, neg, 0.0)
    return s


def _attn_probs(s, sink):
    m = jnp.maximum(jnp.max(s, axis=-1, keepdims=True), sink)
    return jnp.exp2(s - m).astype(BF16), jnp.exp2(sink - m)


def _attn_values(p, pr, kv_scr):
    r0 = p * BLOCK
    return _dot(pr, kv_scr[r0:r0 + BAND, LANES:3 * LANES])


def _attn_finish(p, o, esink, att_scr):
    lane = lax.broadcasted_iota(jnp.int32, (BLOCK, LANES), 1)
    low = lane < HEAD_DIM
    r0 = p * BLOCK
    on = o[:, :LANES] / (o[:, LANES:] + esink)
    for c in range(N_Q_HEADS // 2):
        a0 = on[(2 * c) * BLOCK:(2 * c + 1) * BLOCK]
        a1 = on[(2 * c + 1) * BLOCK:(2 * c + 2) * BLOCK]
        att_scr[r0:r0 + BLOCK, c * LANES:(c + 1) * LANES] = jnp.where(low, a0, a1).astype(BF16)


def _mem_scores(h, cq, mk_ref):
    hs = slice(h * MEM_HEAD_DIM, (h + 1) * MEM_HEAD_DIM)
    return _dot_nt(cq[:, hs].astype(BF16), mk_ref[0, 0, :, hs])


def _mem_probs(s):
    return jnp.exp2(s - jnp.max(s, axis=-1, keepdims=True)).astype(BF16)


def _mem_finish(h, o, matt_scr):
    hs = slice(h * MEM_HEAD_DIM, (h + 1) * MEM_HEAD_DIM)
    matt_scr[:, hs] = (o[:, :MEM_HEAD_DIM] / o[:, MEM_HEAD_DIM:]).astype(BF16)


def _lru_conv(xr, ls, tail_scr, cw_ref, cb_ref):
    out = []
    for blk in range(xr.shape[0] // BLOCK):
        x = xr[blk * BLOCK:(blk + 1) * BLOCK]
        tails = []
        for m in range(TAPS_BACK):
            k = SEG - TAPS_BACK + m
            base = 2 * SUBLANES * m
            xk = x[k * SUBLANES:(k + 1) * SUBLANES]
            tail_scr[base + SUBLANES:base + 2 * SUBLANES, ls] = xk
            tails.append(tail_scr[base + SUBLANES - 1:base + 2 * SUBLANES - 1, ls])
            tail_scr[base:base + SUBLANES, ls] = xk
        xe = jnp.concatenate(tails + [x], axis=0)
        off = TAPS_BACK * SUBLANES
        xc = cw_ref[TAPS_BACK:CONV_WIDTH, ls] * x + cb_ref[:, ls]
        for tap in range(TAPS_BACK):
            back = TAPS_BACK - tap
            lo = off - back * SUBLANES
            xc = xc + cw_ref[tap:tap + 1, ls] * xe[lo:lo + BLOCK]
        out.append(xc)
    return jnp.concatenate(out, axis=0)


def _lru_scan(a, b, cin):
    rows = [slice(k * SUBLANES, (k + 1) * SUBLANES) for k in range(SEG)]
    hs = [b[rows[0]]]
    ps = [a[rows[0]]]
    for k in range(1, SEG):
        hs.append(a[rows[k]] * hs[-1] + b[rows[k]])
        ps.append(a[rows[k]] * ps[-1])
    sub = lax.broadcasted_iota(jnp.int32, (SUBLANES, 1), 0)
    ap, hp = ps[-1], hs[-1]
    for d in (1, 2, 4):
        keep = sub >= d
        a_sh = jnp.where(keep, pltpu.roll(ap, d, 0), 1.0)
        h_sh = jnp.where(keep, pltpu.roll(hp, d, 0), 0.0)
        hp = ap * h_sh + hp
        ap = ap * a_sh
    keep = sub >= 1
    ax = jnp.where(keep, pltpu.roll(ap, 1, 0), 1.0)
    hx = jnp.where(keep, pltpu.roll(hp, 1, 0), 0.0)
    c = ax * cin + hx
    h = [hs[k] + ps[k] * c for k in range(SEG)]
    return jnp.concatenate(h, axis=0), h[-1][SUBLANES - 1:SUBLANES]


def _lru_pointwise(gidx, blk, first, j, xc, ga, yr, bah_ref, bxh_ref, lam_ref, hc_scr, lru_scr):
    ls = slice(gidx * LRU_GROUP, (gidx + 1) * LRU_GROUP)
    rs = slice(blk * BLOCK, (blk + 1) * BLOCK)
    t_r = jnp.tanh(ga[rs, :LRU_GROUP] + bah_ref[:, ls])
    t_g = jnp.tanh(ga[rs, LRU_GROUP:] + bxh_ref[:, ls])
    z = -lam_ref[:, ls]
    softplus = jnp.maximum(z, 0.0) + jnp.log1p(jnp.exp(-jnp.abs(z)))
    ch = (-0.5 * LRU_C) * softplus
    a = jnp.exp(ch + ch * t_r)
    mult = jnp.sqrt(jnp.maximum(1.0 - a * a, 0.0))
    if first:
        row = lax.broadcasted_iota(jnp.int32, (SUBLANES, 1), 0)
        first_token = jnp.logical_and(row == 0, j == 0)
        mult = jnp.concatenate([jnp.where(first_token, 1.0, mult[:SUBLANES]), mult[SUBLANES:]], axis=0)
    m1 = mult * (0.5 * xc[rs])
    b = m1 + m1 * t_g
    h, cout = _lru_scan(a, b, hc_scr[:, ls])
    hc_scr[:, ls] = cout
    lru_scr[rs, ls] = (h * _gelu_tanh(yr[rs])).astype(BF16)


def _mix_kernel(x_ref, g_ref, win_ref, gbh_ref, abias_ref, sink_ref, wao_ref,
                cw_ref, cb_ref, wbd_ref, bah_ref, bxh_ref, lam_ref, wlo_ref,
                mk_ref, mv_ref, wmo_ref, wout_ref, o_ref,
                kv_scr, tail_scr, hc_scr, lru_scr, att_scr, matt_scr):
    j = pl.program_id(1)
    ts = MIX_ROWS

    @pl.when(j == 0)
    def _():
        kv_scr[0:HIST, :] = jnp.zeros((HIST, 3 * LANES), BF16)
        kv_scr[:, 2 * LANES:] = jnp.ones((HIST + ts, LANES), BF16)
        tail_scr[...] = jnp.zeros(tail_scr.shape, F32)
        hc_scr[...] = jnp.zeros((1, LRU_WIDTH), F32)

    @pl.when(j > 0)
    def _():
        kv_scr[0:HIST, 0:2 * LANES] = kv_scr[ts:ts + HIST, 0:2 * LANES]

    for u in range(ts // MIX_SUB_ROWS):
        rows = pl.ds(u * MIX_SUB_ROWS, MIX_SUB_ROWS)
        _mix_sub_tile(u == 0, j, x_ref.at[0].at[rows], g_ref, win_ref, gbh_ref, abias_ref, sink_ref, wao_ref,
                      cw_ref, cb_ref, wbd_ref, bah_ref, bxh_ref, lam_ref, wlo_ref,
                      mk_ref, mv_ref, wmo_ref, wout_ref, o_ref.at[0].at[rows],
                      kv_scr.at[pl.ds(u * MIX_SUB_ROWS, HIST + MIX_SUB_ROWS)], tail_scr, hc_scr,
                      lru_scr.at[rows], att_scr.at[rows], matt_scr.at[rows])


def _mix_sub_tile(head, j, x_ref, g_ref, win_ref, gbh_ref, abias_ref, sink_ref, wao_ref,
                  cw_ref, cb_ref, wbd_ref, bah_ref, bxh_ref, lam_ref, wlo_ref,
                  mk_ref, mv_ref, wmo_ref, wout_ref, o_ref,
                  kv_scr, tail_scr, hc_scr, lru_scr, att_scr, matt_scr):
    ts = MIX_SUB_ROWS
    x = x_ref[...]
    hn = _rmsnorm(x, g_ref[...]).astype(BF16)

    sink = sink_ref[...]

    def proj(lo, width):
        return _dot(hn, win_ref[:, lo:lo + width])

    def pxr(g):
        return proj(O_XR + g * LRU_GROUP, LRU_GROUP)

    def pyr(g):
        return proj(O_YR + g * LRU_GROUP, LRU_GROUP)

    def lru_gates(g, xr):
        ls = slice(g * LRU_GROUP, (g + 1) * LRU_GROUP)
        xc = _lru_conv(xr, ls, tail_scr, cw_ref, cb_ref)
        return xc, _dot(xc.astype(BF16), wbd_ref[g])

    def lru_out(g, blk, xc, ga, yr):
        _lru_pointwise(g, blk, head and blk == 0, j, xc, ga, yr, bah_ref, bxh_ref, lam_ref, hc_scr, lru_scr)

    def gate_part(idx, c):
        return proj(O_G + idx * D_MODEL + c * MXU_DIM, MXU_DIM)

    def gated(idx, parts, br):
        t = jnp.tanh(jnp.concatenate(parts, axis=1) + gbh_ref[:, idx * D_MODEL:(idx + 1) * D_MODEL])
        return br + t * br

    att = lambda p: _attn_scores(p, head and p == 0, j, q, kv_scr, abias_ref)

    q = proj(O_Q, ATTN_WIDTH) * (HEAD_DIM ** -0.5 * LOG2E)
    kv = proj(O_KV, 2 * KV_WIDTH)
    kv_scr[HIST:HIST + ts, 0:2 * LANES] = kv.astype(BF16)
    n_parts = D_MODEL // MXU_DIM
    xr0, yr0 = pxr(0), pyr(0)
    s0 = att(0)
    xr1, yr1 = pxr(1), pyr(1)
    s1 = att(1)
    xc0, ga0 = lru_gates(0, xr0)
    p0, es0 = _attn_probs(s0, sink)
    cq = proj(O_CQ, MEM_WIDTH) * (MEM_HEAD_DIM ** -0.5 * LOG2E)
    o0 = _attn_values(0, p0, kv_scr)
    p1, es1 = _attn_probs(s1, sink)
    lru_out(0, 0, xc0, ga0, yr0)
    lru_out(0, 1, xc0, ga0, yr0)
    o1 = _attn_values(1, p1, kv_scr)
    xc1, ga1 = lru_gates(1, xr1)
    ms0 = _mem_scores(0, cq, mk_ref)
    ms1 = _mem_scores(1, cq, mk_ref)
    xr2, yr2 = pxr(2), pyr(2)
    _attn_finish(0, o0, es0, att_scr)
    _attn_finish(1, o1, es1, att_scr)
    lru_out(1, 0, xc1, ga1, yr1)
    lru_out(1, 1, xc1, ga1, yr1)
    mp0 = _mem_probs(ms0)
    mp1 = _mem_probs(ms1)
    br_attn = _dot(att_scr[...], wao_ref[...])
    xc2, ga2 = lru_gates(2, xr2)
    mo0 = _dot(mp0, mv_ref[0, 0, 0])
    mo1 = _dot(mp1, mv_ref[0, 0, 1])
    g0 = [gate_part(0, c) for c in range(n_parts)]
    ms2 = _mem_scores(2, cq, mk_ref)
    ms3 = _mem_scores(3, cq, mk_ref)
    lru_out(2, 0, xc2, ga2, yr2)
    lru_out(2, 1, xc2, ga2, yr2)
    xr3, yr3 = pxr(3), pyr(3)
    merged = gated(0, g0, br_attn)
    _mem_finish(0, mo0, matt_scr)
    _mem_finish(1, mo1, matt_scr)
    mp2 = _mem_probs(ms2)
    mp3 = _mem_probs(ms3)
    xc3, ga3 = lru_gates(3, xr3)
    mo2 = _dot(mp2, mv_ref[0, 0, 2])
    mo3 = _dot(mp3, mv_ref[0, 0, 3])
    g2 = [gate_part(2, c) for c in range(n_parts)]
    lru_out(3, 0, xc3, ga3, yr3)
    lru_out(3, 1, xc3, ga3, yr3)
    _mem_finish(2, mo2, matt_scr)
    _mem_finish(3, mo3, matt_scr)
    g1 = [gate_part(1, c) for c in range(n_parts)]
    br_mem = _dot(matt_scr[...], wmo_ref[...])
    merged = merged + gated(2, g2, br_mem)
    br_lru = _dot(lru_scr[...], wlo_ref[...])
    merged = merged + gated(1, g1, br_lru)

    o_ref[...] = x + _dot(merged.astype(BF16), wout_ref[...])


def _mix_call(x, layer, g, win, gbh, abias, sink, wao, cw, cb, wbd, bah, bxh, lam, wlo,
              memk, memv, wmo, wout):
    b, s, d = x.shape
    ts = MIX_ROWS
    tile = pl.BlockSpec((1, ts, d), lambda i, j: (i, j, 0))
    m = memk.shape[2]
    per_layer = functools.partial(_layer_resident, layer=layer)
    in_specs = [
        tile,
        per_layer(g), per_layer(win), per_layer(gbh), _resident(abias.shape),
        per_layer(sink), per_layer(wao), per_layer(cw), per_layer(cb),
        per_layer(wbd), per_layer(bah), per_layer(bxh), per_layer(lam),
        per_layer(wlo),
        pl.BlockSpec((1, 1, m, MEM_WIDTH), lambda i, j: (layer, i, 0, 0)),
        pl.BlockSpec((1, 1, MEM_HEADS, m, 2 * MEM_HEAD_DIM), lambda i, j: (layer, i, 0, 0, 0)),
        per_layer(wmo), per_layer(wout),
    ]
    scratch = [
        pltpu.VMEM((HIST + ts, 3 * LANES), BF16),
        pltpu.VMEM((2 * SUBLANES * TAPS_BACK, LRU_WIDTH), F32),
        pltpu.VMEM((1, LRU_WIDTH), F32),
        pltpu.VMEM((ts, LRU_WIDTH), BF16),
        pltpu.VMEM((ts, ATTN_WIDTH), BF16),
        pltpu.VMEM((ts, MEM_WIDTH), BF16),
    ]
    return pl.pallas_call(
        _mix_kernel,
        grid=(b, s // ts),
        in_specs=in_specs,
        out_specs=tile,
        out_shape=jax.ShapeDtypeStruct((b, s, d), F32),
        scratch_shapes=scratch,
        compiler_params=pltpu.CompilerParams(
            dimension_semantics=("arbitrary", "arbitrary"), vmem_limit_bytes=VMEM_LIMIT_BYTES),
        name="mixer",
    )(x, g, win, gbh, abias, sink, wao, cw, cb, wbd, bah, bxh, lam, wlo, memk, memv, wmo, wout)


def _row_token():
    r = np.arange(BLOCK)
    return SEG * (r % SUBLANES) + r // SUBLANES


def _alibi_bias():
    tq = _row_token()[:, None]
    c = np.arange(BAND)
    tk = (BLOCK * (c // BLOCK) + _row_token()[c % BLOCK])[None, :]
    dist = np.abs(tq + HIST - tk).astype(np.float32)
    kc = tk // CHUNK
    valid = (kc >= tq // CHUNK) & (kc <= tq // CHUNK + WIN_CHUNKS)
    out = np.zeros((N_Q_HEADS, BLOCK, BAND), np.float32)
    for i, h in enumerate(HEAD_PERM):
        slope = np.float32(2.0 ** (-8.0 * (h + 1) / N_Q_HEADS))
        out[i] = np.where(valid, -(slope * dist) * np.float32(LOG2E), -np.inf)
    return jnp.asarray(out.reshape(N_Q_HEADS * BLOCK, BAND))


def _block_diag_groups(w):
    depth = w.shape[0]
    w = w.reshape(depth, N_LRU_GROUPS, BLOCKS_PER_GROUP, LRU_BLOCK, LRU_BLOCK)
    eye = jnp.eye(BLOCKS_PER_GROUP, dtype=w.dtype)
    bd = jnp.einsum("lgbij,bc->lgbicj", w, eye)
    return bd.reshape(depth, N_LRU_GROUPS, LRU_GROUP, LRU_GROUP)


def _interleave(x, inverse=False):
    b, s, d = x.shape
    shape = (b, s // BLOCK, SEG, SUBLANES, d) if inverse else (b, s // BLOCK, SUBLANES, SEG, d)
    return x.reshape(shape).transpose(0, 1, 3, 2, 4).reshape(b, s, d)


def kernel(x, mem, ffn1_norm, ffn1_w_gate, ffn1_w_up, ffn1_w_down, mix_norm, w_in, gate_bias,
           attn_sinks, w_attn_out, conv_w, conv_b, lru_wa, lru_ba, lru_wx, lru_bx, lru_lambda,
           w_lru_out, mem_norm, w_mem_kv, w_mem_out, w_out, ffn2_norm, ffn2_w_gate, ffn2_w_up,
           ffn2_w_down, final_norm):
    b, s, d = x.shape
    depth = w_in.shape[0]
    assert d == D_MODEL and w_in.shape[2] == IN_WIDTH
    assert s % MIX_ROWS == 0 and (b * s) % FFN_ROWS == 0
    assert MIX_ROWS % MIX_SUB_ROWS == 0 and MIX_SUB_ROWS % BLOCK == 0 and FFN_ROWS % FFN_SUB_ROWS == 0

    def row(v):
        return v.reshape(depth, 1, -1)

    perm = jnp.asarray(HEAD_PERM)
    wq = w_in[:, :, :ATTN_WIDTH].reshape(depth, d, N_Q_HEADS, HEAD_DIM)[:, :, perm].reshape(depth, d, ATTN_WIDTH)
    win = jnp.concatenate([wq, w_in[:, :, ATTN_WIDTH:O_G], 0.5 * w_in[:, :, O_G:]], axis=2).astype(BF16)
    wao = w_attn_out.reshape(depth, N_Q_HEADS, HEAD_DIM, d)[:, perm].reshape(depth, ATTN_WIDTH, d).astype(BF16)
    sink = jnp.repeat(LOG2E * attn_sinks[:, perm], BLOCK, axis=1).reshape(depth, N_Q_HEADS * BLOCK, 1)
    wbd = (0.5 * jnp.concatenate([_block_diag_groups(lru_wa), _block_diag_groups(lru_wx)], axis=3)).astype(BF16)
    mix_params = (row(mix_norm), win, row(0.5 * gate_bias), _alibi_bias(), sink, wao, conv_w, row(conv_b), wbd,
                  row(0.5 * lru_ba), row(0.5 * lru_bx), row(lru_lambda), w_lru_out.astype(BF16))
    wmo, wout = w_mem_out.astype(BF16), (0.5 * w_out).astype(BF16)
    half = lambda w: (0.5 * w).astype(BF16)
    ffn1 = (row(ffn1_norm), half(ffn1_w_gate), ffn1_w_up.astype(BF16), half(ffn1_w_down))
    ffn2 = (row(ffn2_norm), half(ffn2_w_gate), ffn2_w_up.astype(BF16), half(ffn2_w_down))
    fg = final_norm.reshape(1, d)

    memk, memv = _memkv_call(mem, row(mem_norm), w_mem_kv.astype(BF16))
    x = _interleave(x)
    for l in range(depth):
        x = _ffn_call(x.reshape(b * s, d), l, *ffn1, fg, False).reshape(b, s, d)
        x = _mix_call(x, l, *mix_params, memk, memv, wmo, wout)
        x = _ffn_call(x.reshape(b * s, d), l, *ffn2, fg, l == depth - 1).reshape(b, s, d)
    return _interleave(x, inverse=True)
```
